```python
import math
import jax
import jax.numpy as jnp
from jax import lax
import numpy as np

D_MODEL = 1024
BATCH = 2
SEQ = 16384
DEPTH = 2

CHUNK = 64
HEAD_DIM = 64
N_BRANCH = 4
BRANCH_WIDTH = D_MODEL // N_BRANCH
CONV_WIDTH = 3
SC_WIDTH = BRANCH_WIDTH
SB_HEADS = BRANCH_WIDTH // HEAD_DIM
SB_WIDTH = SB_HEADS * HEAD_DIM
SB_BLOCK = 128
RW_HEADS = BRANCH_WIDTH // HEAD_DIM
RW_WIDTH = RW_HEADS * HEAD_DIM
RW_DECAY_LORA = 64
RW_A_LORA = 64
RW_GATE_LORA = 128
RW_GN_EPS = 64e-5
RW_DECAY_SCALE = math.exp(-0.5)
SG_GROUPS = 4
SG_WIDTH = BRANCH_WIDTH
SG_GROUP_WIDTH = SG_WIDTH // SG_GROUPS
SG_CHUNK = 128
D_FF = ((8 * D_MODEL // 3 + 127) // 128) * 128
RMS_EPS = 1e-6
LN_EPS = 1e-5

SC_OFF = 0
SB_OFF = SC_OFF + 3 * SC_WIDTH
RW_OFF = SB_OFF + 3 * SB_WIDTH
RW_COLS = 3 * RW_WIDTH + RW_DECAY_LORA + RW_A_LORA + RW_GATE_LORA
SG_OFF = RW_OFF + RW_COLS
GATE_OFF = SG_OFF + 2 * SG_WIDTH
IN_COLS = GATE_OFF + N_BRANCH * D_MODEL

kernel_name = "hybrid_gated_parallel_streaming_encoder"


def rms_norm(x, g):
    xf = x.astype(jnp.float32)
    y = xf * lax.rsqrt(jnp.mean(xf * xf, axis=-1, keepdims=True) + RMS_EPS)
    return (y * g.astype(jnp.float32)).astype(x.dtype)


def layer_norm(x, g, b):
    xf = x.astype(jnp.float32)
    mu = jnp.mean(xf, axis=-1, keepdims=True)
    var = jnp.mean(jnp.square(xf - mu), axis=-1, keepdims=True)
    y = (xf - mu) * lax.rsqrt(var + LN_EPS) * g.astype(jnp.float32) + b.astype(jnp.float32)
    return y.astype(x.dtype)


def causal_dwconv(x, w):
    k = w.shape[0]
    t = x.shape[1]
    xp = jnp.pad(x, ((0, 0), (k - 1, 0), (0, 0)))
    out = xp[:, 0:t] * w[0]
    for i in range(1, k):
        out = out + xp[:, i:i + t] * w[i]
    return out


def token_shift(x):
    return jnp.pad(x, ((0, 0), (1, 0), (0, 0)))[:, :-1]


def short_conv_mixer(z, conv_w):
    b_gate, c_gate, xin = jnp.split(z, 3, axis=-1)
    return b_gate * causal_dwconv(c_gate * xin, conv_w)


def stick_breaking_attention(q, k, v):
    bsz, t, h, dh = q.shape
    nb = t // SB_BLOCK
    scale = 1.0 / math.sqrt(dh)
    qb = q.reshape(bsz, nb, SB_BLOCK, h, dh).transpose(1, 0, 3, 2, 4)
    kh = k.transpose(0, 2, 1, 3)
    vh = v.transpose(0, 2, 1, 3)
    kpos = jnp.arange(t)

    def block(args):
        qi, bi = args
        z = jnp.einsum('bhqd,bhkd->bhqk', qi, kh).astype(jnp.float32) * scale
        qpos = bi * SB_BLOCK + jnp.arange(SB_BLOCK)
        mask = kpos[None, :] < qpos[:, None]
        log_beta = jax.nn.log_sigmoid(z)
        log_keep = jnp.where(mask, log_beta - z, 0.0)
        suffix = lax.cumsum(log_keep, axis=3, reverse=True) - log_keep
        attn = jnp.where(mask, jnp.exp(log_beta + suffix), 0.0)
        return jnp.einsum('bhqk,bhkd->bhqd', attn.astype(vh.dtype), vh)

    out = lax.map(block, (qb, jnp.arange(nb)))
    return out.transpose(1, 0, 3, 2, 4).reshape(bsz, t, h * dh)


def rwkv7_scan(r, w, k, v, a_vec, b_vec):
    bsz, t, h, n = r.shape

    def step(s, inp):
        r_t, w_t, k_t, v_t, a_t, b_t = inp
        sa = jnp.einsum('bhvk,bhk->bhv', s, a_t)
        s = s * w_t[:, :, None, :] + sa[..., None] * b_t[:, :, None, :] + v_t[..., None] * k_t[:, :, None, :]
        return s, jnp.einsum('bhvk,bhk->bhv', s, r_t)

    xs = tuple(u.astype(jnp.float32).transpose(1, 0, 2, 3) for u in (r, w, k, v, a_vec, b_vec))
    s0 = jnp.zeros((bsz, h, n, n), jnp.float32)
    _, y = lax.scan(step, s0, xs)
    return y.transpose(1, 0, 2, 3)


def rwkv7_mixer(z, mu, w0, w2, a0, a2, g2, k_k, k_a, r_k, gn_g, gn_b):
    bsz, t, _ = z.shape
    z = z + (token_shift(z) - z) * mu
    c0, c1, c2 = RW_WIDTH, 2 * RW_WIDTH, 3 * RW_WIDTH
    c3, c4 = c2 + RW_DECAY_LORA, c2 + RW_DECAY_LORA + RW_A_LORA
    r, k, v = z[..., :c0], z[..., c0:c1], z[..., c1:c2]
    xw, xa, xg = z[..., c2:c3], z[..., c3:c4], z[..., c4:]
    decay = jnp.exp(-RW_DECAY_SCALE * jax.nn.sigmoid((w0 + jnp.tanh(xw) @ w2).astype(jnp.float32)))
    a = jax.nn.sigmoid(a0 + xa @ a2)
    g = jax.nn.sigmoid(xg) @ g2
    hs = (bsz, t, RW_HEADS, HEAD_DIM)
    kk = (k * k_k).reshape(hs).astype(jnp.float32)
    kk = kk / jnp.maximum(jnp.sqrt(jnp.sum(kk * kk, axis=-1, keepdims=True)), 1e-12)
    k = k * (1.0 + (a - 1.0) * k_a)
    rh, kh, vh, ah = r.reshape(hs), k.reshape(hs), v.reshape(hs), a.reshape(hs)
    y = rwkv7_scan(rh, decay.reshape(hs), kh, vh, -kk, kk * ah.astype(jnp.float32))
    mean = jnp.mean(y, axis=-1, keepdims=True)
    var = jnp.mean(jnp.square(y - mean), axis=-1, keepdims=True)
    y = ((y - mean) * lax.rsqrt(var + RW_GN_EPS)).reshape(bsz, t, RW_WIDTH)
    y = y * gn_g.astype(jnp.float32) + gn_b.astype(jnp.float32)
    bonus = jnp.sum((rh * kh * r_k).astype(jnp.float32), axis=-1, keepdims=True) * vh.astype(jnp.float32)
    y = (y + bonus.reshape(bsz, t, RW_WIDTH)).astype(z.dtype)
    return y * g


def spatial_gating(z, ln_g, ln_b, w_s, b_s):
    bsz, t, _ = z.shape
    u, v = jnp.split(jax.nn.gelu(z), 2, axis=-1)
    v = layer_norm(v, ln_g, ln_b)
    vb = v.reshape(bsz, t // SG_CHUNK, SG_CHUNK, SG_GROUPS, SG_GROUP_WIDTH)
    w_causal = jnp.tril(w_s)
    mixed = jnp.einsum('gts,bnsgc->bntgc', w_causal, vb) + b_s.T[:, :, None]
    return u * mixed.reshape(bsz, t, SG_WIDTH)


def hybrid_mixer(h, w_in, gate_b, sc_conv_w, sc_out, sb_out, rw_mu, rw_w0, rw_w2, rw_a0, rw_a2,
                 rw_g2, rw_k_k, rw_k_a, rw_r_k, rw_gn_g, rw_gn_b, rw_out, sg_ln_g, sg_ln_b,
                 sg_w, sg_b, sg_out, w_o):
    bsz, t, _ = h.shape
    proj = h @ w_in
    y_a = short_conv_mixer(proj[..., SC_OFF:SB_OFF], sc_conv_w)
    q, k, v = jnp.split(proj[..., SB_OFF:RW_OFF], 3, axis=-1)
    hs = (bsz, t, SB_HEADS, HEAD_DIM)
    y_b = stick_breaking_attention(q.reshape(hs), k.reshape(hs), v.reshape(hs))
    y_c = rwkv7_mixer(proj[..., RW_OFF:SG_OFF], rw_mu, rw_w0, rw_w2, rw_a0, rw_a2, rw_g2,
                      rw_k_k, rw_k_a, rw_r_k, rw_gn_g, rw_gn_b)
    y_d = spatial_gating(proj[..., SG_OFF:GATE_OFF], sg_ln_g, sg_ln_b, sg_w, sg_b)
    gates = jax.nn.sigmoid((proj[..., GATE_OFF:] + gate_b).astype(jnp.float32)).astype(h.dtype)
    gates = gates.reshape(bsz, t, N_BRANCH, D_MODEL)
    merged = (gates[:, :, 0] * (y_a @ sc_out) + gates[:, :, 1] * (y_b @ sb_out)
              + gates[:, :, 2] * (y_c @ rw_out) + gates[:, :, 3] * (y_d @ sg_out))
    return merged @ w_o


def conv_glu_ffn(h, w_up, conv_w, w_down):
    up = causal_dwconv(h @ w_up, conv_w)
    gate, val = jnp.split(up, 2, axis=-1)
    return (jax.nn.silu(gate) * val) @ w_down


def setup_inputs(seed: int = 0) -> dict:
    key = jax.random.key(seed)
    keys = iter(jax.random.split(key, 40))
    f32 = jnp.float32
    L = DEPTH

    def nrm(shape, scale):
        return jax.random.normal(next(keys), shape, f32) * scale

    return {
        "x": nrm((BATCH, SEQ, D_MODEL), 1.0),
        "mix_norm_g": 1.0 + nrm((L, D_MODEL), 0.05),
        "w_in": nrm((L, D_MODEL, IN_COLS), D_MODEL ** -0.5),
        "gate_b": nrm((L, N_BRANCH * D_MODEL), 0.1),
        "sc_conv_w": nrm((L, CONV_WIDTH, SC_WIDTH), CONV_WIDTH ** -0.5),
        "sc_out": nrm((L, SC_WIDTH, D_MODEL), SC_WIDTH ** -0.5),
        "sb_out": nrm((L, SB_WIDTH, D_MODEL), SB_WIDTH ** -0.5),
        "rw_mu": jax.random.uniform(next(keys), (L, RW_COLS), f32),
        "rw_w0": nrm((L, RW_WIDTH), 0.5),
        "rw_w2": nrm((L, RW_DECAY_LORA, RW_WIDTH), 0.1),
        "rw_a0": nrm((L, RW_WIDTH), 0.1),
        "rw_a2": nrm((L, RW_A_LORA, RW_WIDTH), 0.1),
        "rw_g2": nrm((L, RW_GATE_LORA, RW_WIDTH), RW_GATE_LORA ** -0.5),
        "rw_k_k": 0.85 + nrm((L, RW_WIDTH), 0.05),
        "rw_k_a": 1.0 + nrm((L, RW_WIDTH), 0.05),
        "rw_r_k": nrm((L, RW_HEADS, HEAD_DIM), 0.1),
        "rw_gn_g": 1.0 + nrm((L, RW_WIDTH), 0.05),
        "rw_gn_b": nrm((L, RW_WIDTH), 0.01),
        "rw_out": nrm((L, RW_WIDTH, D_MODEL), RW_WIDTH ** -0.5),
        "sg_ln_g": 1.0 + nrm((L, SG_WIDTH), 0.05),
        "sg_ln_b": nrm((L, SG_WIDTH), 0.01),
        "sg_w": nrm((L, SG_GROUPS, SG_CHUNK, SG_CHUNK), 0.05),
        "sg_b": 1.0 + nrm((L, SG_GROUPS, SG_CHUNK), 0.05),
        "sg_out": nrm((L, SG_WIDTH, D_MODEL), SG_WIDTH ** -0.5),
        "w_o": nrm((L, D_MODEL, D_MODEL), 0.5 * D_MODEL ** -0.5),
        "ffn_norm_g": 1.0 + nrm((L, D_MODEL), 0.05),
        "w_up": nrm((L, D_MODEL, 2 * D_FF), D_MODEL ** -0.5),
        "ffn_conv_w": nrm((L, CONV_WIDTH, 2 * D_FF), CONV_WIDTH ** -0.5),
        "w_down": nrm((L, D_FF, D_MODEL), 0.5 * D_FF ** -0.5),
        "final_norm_g": 1.0 + nrm((D_MODEL,), 0.05),
    }


def reference(x, mix_norm_g, w_in, gate_b, sc_conv_w, sc_out, sb_out, rw_mu, rw_w0, rw_w2,
              rw_a0, rw_a2, rw_g2, rw_k_k, rw_k_a, rw_r_k, rw_gn_g, rw_gn_b, rw_out, sg_ln_g,
              sg_ln_b, sg_w, sg_b, sg_out, w_o, ffn_norm_g, w_up, ffn_conv_w, w_down,
              final_norm_g):
    for l in range(DEPTH):
        h = rms_norm(x, mix_norm_g[l])
        x = x + hybrid_mixer(h, w_in[l], gate_b[l], sc_conv_w[l], sc_out[l], sb_out[l], rw_mu[l],
                             rw_w0[l], rw_w2[l], rw_a0[l], rw_a2[l], rw_g2[l], rw_k_k[l],
                             rw_k_a[l], rw_r_k[l], rw_gn_g[l], rw_gn_b[l], rw_out[l], sg_ln_g[l],
                             sg_ln_b[l], sg_w[l], sg_b[l], sg_out[l], w_o[l])
        h = rms_norm(x, ffn_norm_g[l])
        x = x + conv_glu_ffn(h, w_up[l], ffn_conv_w[l], w_down[l])
    return rms_norm(x, final_norm_g)
```

```python
import functools
import math

import jax
import jax.numpy as jnp
from jax import lax
from jax.experimental import pallas as pl
from jax.experimental.pallas import tpu as pltpu

F32 = jnp.float32
BF16 = jnp.bfloat16

D_MODEL = 1024
HEAD_DIM = 64
N_HEADS = 4
BRANCH = 256
D_FF = 2816
RMS_EPS = 1e-6
LN_EPS = 1e-5
RW_GN_EPS = 64e-5
RW_DECAY_SCALE = math.exp(-0.5)
SG_CHUNK = 128

SC_OFF, SB_OFF, RW_OFF, SG_OFF, GATE_OFF = 0, 768, 1536, 2560, 3072

ROW_TILE = 512
SB_BLOCK = 128
RW_TILE = 256
RW_CHUNK = 64
FF_CHUNK = 256
SB_LOG_CUTOFF = -104.0
VMEM_LIMIT = 52 * 1024 * 1024


def _dot(a, b):
    return jnp.dot(a, b, preferred_element_type=F32)


def _dot_split_lhs(x, m, passes=3):
    acc = None
    rem = x
    for _ in range(passes):
        hi = rem.astype(BF16)
        d = _dot(hi, m)
        acc = d if acc is None else acc + d
        rem = rem - hi.astype(F32)
    return acc


def _dot_split_rhs(m, x, passes=3):
    acc = None
    rem = x
    for _ in range(passes):
        hi = rem.astype(BF16)
        d = _dot(m, hi)
        acc = d if acc is None else acc + d
        rem = rem - hi.astype(F32)
    return acc


def _rms_norm(x, g):
    return x * lax.rsqrt(jnp.mean(x * x, axis=-1, keepdims=True) + RMS_EPS) * g


def _shift_rows(u, prev8, s):
    r = pltpu.roll(u, s, 0)
    c = pltpu.roll(prev8, s, 0)
    row = lax.broadcasted_iota(jnp.int32, c.shape, 0)
    top = jnp.where(row < s, c, r[0:8])
    return jnp.concatenate([top, r[8:]], axis=0)


def _causal_conv3(u, prev8, w):
    return (_shift_rows(u, prev8, 2) * w[0:1] + _shift_rows(u, prev8, 1) * w[1:2]
            + u * w[2:3])


def _gelu_tanh(x):
    return 0.5 * x * (1.0 + jnp.tanh(math.sqrt(2.0 / math.pi) * (x + 0.044715 * x * x * x)))


def _head_of_lane(shape, axis):
    return lax.broadcasted_iota(jnp.int32, shape, axis) // HEAD_DIM


def _inproj_kernel(x_ref, g_ref, w_ref, cw_ref, lng_ref, lnb_ref, sgw_ref, sgb_ref,
                   ya_ref, yd_ref, q_ref, kt_ref, v_ref, rw_ref, carry_ref):
    tm = x_ref.shape[1]

    @pl.when(pl.program_id(1) == 0)
    def _():
        carry_ref[...] = jnp.zeros_like(carry_ref)

    h = _rms_norm(x_ref[0], g_ref[...]).astype(BF16)

    p = _dot(h, w_ref[:, SC_OFF:SC_OFF + 3 * BRANCH])
    u = p[:, BRANCH:2 * BRANCH] * p[:, 2 * BRANCH:3 * BRANCH]
    conv = _causal_conv3(u, carry_ref[...], cw_ref[...])
    carry_ref[...] = u[tm - 8:tm]
    ya_ref[0] = (p[:, 0:BRANCH] * conv).astype(BF16)

    q = _dot(h, w_ref[:, SB_OFF:SB_OFF + BRANCH]) * (1.0 / math.sqrt(HEAD_DIM))
    q_ref[0] = q.astype(BF16)
    k = _dot(h, w_ref[:, SB_OFF + BRANCH:SB_OFF + 2 * BRANCH])
    for c in range(tm // SB_BLOCK):
        kt_ref[0, c] = k[c * SB_BLOCK:(c + 1) * SB_BLOCK].T.astype(BF16)
    v_ref[0] = _dot(h, w_ref[:, SB_OFF + 2 * BRANCH:SB_OFF + 3 * BRANCH]).astype(BF16)

    rw_ref[0] = _dot(h, w_ref[:, RW_OFF:SG_OFF])

    z = _gelu_tanh(_dot(h, w_ref[:, SG_OFF:GATE_OFF]))
    u_g = z[:, 0:BRANCH]
    vv = z[:, BRANCH:2 * BRANCH]
    mu = jnp.mean(vv, axis=-1, keepdims=True)
    vc = vv - mu
    var = jnp.mean(vc * vc, axis=-1, keepdims=True)
    vn = (vc * lax.rsqrt(var + LN_EPS) * lng_ref[...] + lnb_ref[...]).astype(BF16)
    tri = (lax.broadcasted_iota(jnp.int32, (SG_CHUNK, SG_CHUNK), 1)
           <= lax.broadcasted_iota(jnp.int32, (SG_CHUNK, SG_CHUNK), 0))
    grp = _head_of_lane((1, BRANCH), 1)
    ws = [jnp.where(tri, sgw_ref[g], 0.0).astype(BF16) for g in range(N_HEADS)]
    for c in range(tm // SG_CHUNK):
        rows = slice(c * SG_CHUNK, (c + 1) * SG_CHUNK)
        vch = vn[rows]
        mixed = sgb_ref[...]
        for g in range(N_HEADS):
            mixed = mixed + jnp.where(grp == g, _dot(ws[g], vch), 0.0)
        yd_ref[0, rows, :] = (u_g[rows] * mixed).astype(BF16)


def _inproj_call(x, g, w, cw, lng, lnb, sgw, sgb):
    bsz, t, d = x.shape
    tm = min(ROW_TILE, t)
    nkb = t // SB_BLOCK
    const2 = lambda b, i: (0, 0)
    tok = lambda width: pl.BlockSpec((1, tm, width), lambda b, i: (b, i, 0))
    return pl.pallas_call(
        _inproj_kernel,
        grid=(bsz, t // tm),
        in_specs=[
            tok(d),
            pl.BlockSpec((1, d), const2),
            pl.BlockSpec(w.shape, const2),
            pl.BlockSpec(cw.shape, const2),
            pl.BlockSpec((1, BRANCH), const2),
            pl.BlockSpec((1, BRANCH), const2),
            pl.BlockSpec(sgw.shape, lambda b, i: (0, 0, 0)),
            pl.BlockSpec(sgb.shape, const2),
        ],
        out_specs=[
            tok(BRANCH), tok(BRANCH), tok(BRANCH),
            pl.BlockSpec((1, tm // SB_BLOCK, BRANCH, SB_BLOCK), lambda b, i: (b, i, 0, 0)),
            tok(BRANCH), tok(4 * BRANCH),
        ],
        out_shape=[
            jax.ShapeDtypeStruct((bsz, t, BRANCH), BF16),
            jax.ShapeDtypeStruct((bsz, t, BRANCH), BF16),
            jax.ShapeDtypeStruct((bsz, t, BRANCH), BF16),
            jax.ShapeDtypeStruct((bsz, nkb, BRANCH, SB_BLOCK), BF16),
            jax.ShapeDtypeStruct((bsz, t, BRANCH), BF16),
            jax.ShapeDtypeStruct((bsz, t, 4 * BRANCH), F32),
        ],
        scratch_shapes=[pltpu.VMEM((8, BRANCH), F32)],
        compiler_params=pltpu.CompilerParams(
            dimension_semantics=("arbitrary", "arbitrary"), vmem_limit_bytes=VMEM_LIMIT),
        name="inproj",
    )(x, g, w, cw, lng, lnb, sgw, sgb)


def _sb_kernel(q_ref, kt_ref, v_ref, o_ref):
    bq = q_ref.shape[1]
    i = pl.program_id(1)
    q = q_ref[0]
    lane_head = _head_of_lane((1, BRANCH), 1)
    row = lax.broadcasted_iota(jnp.int32, (bq, SB_BLOCK), 0)
    col = lax.broadcasted_iota(jnp.int32, (bq, SB_BLOCK), 1)
    below = col < row
    later = (lax.broadcasted_iota(jnp.int32, (SB_BLOCK, SB_BLOCK), 0)
             > lax.broadcasted_iota(jnp.int32, (SB_BLOCK, SB_BLOCK), 1)).astype(BF16)

    def block(j, qm, carry, acc, diagonal):
        z = _dot(qm, kt_ref[0, j])
        sp = jnp.maximum(z, 0.0) + jnp.log1p(jnp.exp(-jnp.abs(z)))
        log_keep = -sp
        if diagonal:
            log_keep = jnp.where(below, log_keep, 0.0)
        suffix = _dot_split_lhs(log_keep, later)
        attn = jnp.exp((z - sp) + suffix + carry)
        if diagonal:
            attn = jnp.where(below, attn, 0.0)
        start = pl.multiple_of(j * SB_BLOCK, SB_BLOCK)
        acc = acc + _dot(attn.astype(BF16), v_ref[0, pl.ds(start, SB_BLOCK), :])
        carry = carry + jnp.sum(log_keep, axis=1, keepdims=True)
        return carry, acc

    out = jnp.zeros((bq, BRANCH), F32)
    for h in range(N_HEADS):
        head = lane_head == h
        qm = jnp.where(head, q, jnp.zeros_like(q))
        carry, acc = block(i, qm, jnp.zeros((bq, 1), F32), jnp.zeros((bq, BRANCH), F32), True)

        def cond(state):
            j, carry, _ = state
            return jnp.logical_and(j >= 0, jnp.max(carry) > SB_LOG_CUTOFF)

        def body(state, qm=qm):
            j, carry, acc = state
            carry, acc = block(j, qm, carry, acc, False)
            return j - 1, carry, acc

        _, _, acc = lax.while_loop(cond, body, (i - 1, carry, acc))
        out = jnp.where(head, acc, out)
    o_ref[0] = out.astype(BF16)


def _sb_call(q, kt, v):
    bsz, t, _ = q.shape
    bq = SB_BLOCK
    return pl.pallas_call(
        _sb_kernel,
        grid=(bsz, t // bq),
        in_specs=[
            pl.BlockSpec((1, bq, BRANCH), lambda b, i: (b, i, 0)),
            pl.BlockSpec((1,) + kt.shape[1:], lambda b, i: (b, 0, 0, 0)),
            pl.BlockSpec((1, t, BRANCH), lambda b, i: (b, 0, 0)),
        ],
        out_specs=pl.BlockSpec((1, bq, BRANCH), lambda b, i: (b, i, 0)),
        out_shape=jax.ShapeDtypeStruct((bsz, t, BRANCH), BF16),
        compiler_params=pltpu.CompilerParams(
            dimension_semantics=("arbitrary", "arbitrary"), vmem_limit_bytes=VMEM_LIMIT),
        name="stickbreak",
    )(q, kt, v)


def _rwkv_chunk(r, lw, k2, v, av, bv, state, decay_col, masks):
    head_masks, strict_low, incl_low, same_head, eye, tril_rows = masks
    c = RW_CHUNK
    tile4 = lambda a: jnp.concatenate([a] * N_HEADS, axis=0)
    by_head = lambda a: jnp.concatenate([jnp.where(m, a, 0.0) for m in head_masks], axis=0)
    pick = lambda a4: sum(jnp.where(m, a4[h * c:(h + 1) * c], 0.0)
                          for h, m in enumerate(head_masks))

    cum = _dot_split_rhs(tril_rows, lw)
    total = cum[c - 1:c]
    e_in = jnp.exp(cum)
    r_in = r * e_in
    a_in = av * e_in * jnp.exp(-lw)
    e_end = jnp.exp(-total)
    e_out = jnp.exp(total - cum)
    b_out = bv * e_out
    k_out = k2 * e_out

    lhs = jnp.concatenate([by_head(a_in * e_end), by_head(r_in * e_end)], axis=0).astype(BF16)
    bt4 = tile4(b_out).T.astype(BF16)
    kt4 = tile4(k_out).T.astype(BF16)
    gram_b = _dot(lhs, bt4)
    gram_k = _dot(lhs, kt4)
    n = N_HEADS * c
    a_ab = jnp.where(strict_low, gram_b[:n], 0.0)
    m_rb = jnp.where(incl_low, gram_b[n:], 0.0)
    a_ak = jnp.where(strict_low, gram_k[:n], 0.0)
    m_rk = jnp.where(incl_low, gram_k[n:], 0.0)

    inv = eye + a_ab
    power = a_ab
    for _ in range(int(math.log2(c)) - 1):
        pb = power.astype(BF16)
        power = _dot(pb, pb)
        inv = inv + _dot(inv.astype(BF16), power.astype(BF16))

    v4 = tile4(v).astype(BF16)
    akv = _dot(a_ak.astype(BF16), v4)
    sb = state.astype(BF16)
    from_state = _dot(jnp.concatenate([a_in, r_in], axis=0).astype(BF16), sb)
    u4 = _dot(inv.astype(BF16), (tile4(from_state[:c]) + akv).astype(BF16))
    u = pick(u4)
    y4 = _dot(m_rb.astype(BF16), tile4(u).astype(BF16)) + _dot(m_rk.astype(BF16), v4)
    y = from_state[c:] + pick(y4)
    upd = _dot(bt4, by_head(u).astype(BF16)) + _dot(kt4, by_head(v).astype(BF16))
    new_state = state * decay_col + jnp.where(same_head, upd, 0.0)
    return y, new_state


def _rwkv_kernel(p_ref, mu_ref, w0_ref, w2_ref, a0_ref, a2_ref, g2_ref, kk_ref, ka_ref,
                 rk_ref, gng_ref, gnb_ref, o_ref, prev_ref, state_ref):
    tt = p_ref.shape[1]
    c = RW_CHUNK
    n = N_HEADS * c

    @pl.when(pl.program_id(1) == 0)
    def _():
        prev_ref[...] = jnp.zeros_like(prev_ref)
        state_ref[...] = jnp.zeros_like(state_ref)

    z = p_ref[0]
    zs = z + (_shift_rows(z, prev_ref[...], 1) - z) * mu_ref[...]
    prev_ref[...] = z[tt - 8:tt]
    r = zs[:, 0:BRANCH]
    k = zs[:, BRANCH:2 * BRANCH]
    v = zs[:, 2 * BRANCH:3 * BRANCH]
    xwa = zs[:, 3 * BRANCH:3 * BRANCH + 128]
    xg = zs[:, 3 * BRANCH + 128:4 * BRANCH]
    lw = -RW_DECAY_SCALE * jax.nn.sigmoid(
        w0_ref[...] + _dot(jnp.tanh(xwa).astype(BF16), w2_ref[...]))
    a = jax.nn.sigmoid(a0_ref[...] + _dot(xwa.astype(BF16), a2_ref[...]))
    g = _dot(jax.nn.sigmoid(xg).astype(BF16), g2_ref[...])

    r_i = lax.broadcasted_iota(jnp.int32, (n, n), 0)
    c_i = lax.broadcasted_iota(jnp.int32, (n, n), 1)
    same_head = (r_i // HEAD_DIM) == (c_i // HEAD_DIM)
    ones_head = same_head.astype(BF16)
    head_sum = lambda a_: _dot_split_lhs(a_, ones_head)

    kk = k * kk_ref[...]
    kk = kk / jnp.maximum(jnp.sqrt(head_sum(kk * kk)), 1e-12)
    k2 = k * (1.0 + (a - 1.0) * ka_ref[...])
    av = -kk
    bv = kk * a

    lane_head = _head_of_lane((1, BRANCH), 1)
    masks = (
        [lane_head == h for h in range(N_HEADS)],
        jnp.logical_and(same_head, c_i < r_i),
        jnp.logical_and(same_head, c_i <= r_i),
        same_head,
        (r_i == c_i).astype(F32),
        (lax.broadcasted_iota(jnp.int32, (c, c), 1)
         <= lax.broadcasted_iota(jnp.int32, (c, c), 0)).astype(BF16),
    )

    sel = (lax.broadcasted_iota(jnp.int32, (128, tt), 1) // c
           == lax.broadcasted_iota(jnp.int32, (128, tt), 0)).astype(BF16)
    decay_cols = jnp.exp(_dot_split_rhs(sel, lw).T)

    state = state_ref[...]
    ys = []
    for j in range(tt // c):
        rows = slice(j * c, (j + 1) * c)
        y, state = _rwkv_chunk(r[rows], lw[rows], k2[rows], v[rows], av[rows], bv[rows],
                               state, decay_cols[:, j:j + 1], masks)
        ys.append(y)
    state_ref[...] = state
    y = jnp.concatenate(ys, axis=0)

    inv_n = 1.0 / HEAD_DIM
    yc = y - head_sum(y) * inv_n
    var = head_sum(yc * yc) * inv_n
    yn = yc * lax.rsqrt(var + RW_GN_EPS) * gng_ref[...] + gnb_ref[...]
    bonus = head_sum(r * k2 * rk_ref[...]) * v
    o_ref[0] = ((yn + bonus) * g).astype(BF16)


def _rwkv_call(p, mu, w0, w2p, a0, a2p, g2, k_k, k_a, r_k, gn_g, gn_b):
    bsz, t, width = p.shape
    tt = min(RW_TILE, t)
    const2 = lambda b, i: (0, 0)
    vec = pl.BlockSpec((1, BRANCH), const2)
    return pl.pallas_call(
        _rwkv_kernel,
        grid=(bsz, t // tt),
        in_specs=[
            pl.BlockSpec((1, tt, width), lambda b, i: (b, i, 0)),
            pl.BlockSpec((1, width), const2),
            vec, pl.BlockSpec(w2p.shape, const2),
            vec, pl.BlockSpec(a2p.shape, const2),
            pl.BlockSpec(g2.shape, const2),
            vec, vec, vec, vec, vec,
        ],
        out_specs=pl.BlockSpec((1, tt, BRANCH), lambda b, i: (b, i, 0)),
        out_shape=jax.ShapeDtypeStruct((bsz, t, BRANCH), BF16),
        scratch_shapes=[pltpu.VMEM((8, width), F32),
                        pltpu.VMEM((N_HEADS * RW_CHUNK, BRANCH), F32)],
        compiler_params=pltpu.CompilerParams(
            dimension_semantics=("arbitrary", "arbitrary"), vmem_limit_bytes=VMEM_LIMIT),
        name="rwkv7",
    )(p, mu, w0, w2p, a0, a2p, g2, k_k, k_a, r_k, gn_g, gn_b)


def _merge_kernel(x_ref, g_ref, ya_ref, yb_ref, yc_ref, yd_ref, wg_ref, gb_ref, wout_ref,
                  wo_ref, o_ref):
    x = x_ref[...]
    h = _rms_norm(x, g_ref[...]).astype(BF16)
    merged = None
    for i, y_ref in enumerate((ya_ref, yb_ref, yc_ref, yd_ref)):
        cols = slice(i * D_MODEL, (i + 1) * D_MODEL)
        gate = jax.nn.sigmoid(_dot(h, wg_ref[:, cols]) + gb_ref[:, cols])
        term = gate * _dot(y_ref[...], wout_ref[i])
        merged = term if merged is None else merged + term
    o_ref[...] = x + _dot(merged.astype(BF16), wo_ref[...])


def _merge_call(x, g, ya, yb, yc, yd, wg, gb, wout, wo):
    n, d = x.shape
    tm = min(ROW_TILE, n)
    const2 = lambda i: (0, 0)
    tok = lambda width: pl.BlockSpec((tm, width), lambda i: (i, 0))
    return pl.pallas_call(
        _merge_kernel,
        grid=(n // tm,),
        in_specs=[
            tok(d), pl.BlockSpec((1, d), const2),
            tok(BRANCH), tok(BRANCH), tok(BRANCH), tok(BRANCH),
            pl.BlockSpec(wg.shape, const2), pl.BlockSpec(gb.shape, const2),
            pl.BlockSpec(wout.shape, lambda i: (0, 0, 0)), pl.BlockSpec(wo.shape, const2),
        ],
        out_specs=tok(d),
        out_shape=jax.ShapeDtypeStruct((n, d), F32),
        compiler_params=pltpu.CompilerParams(
            dimension_semantics=("arbitrary",), vmem_limit_bytes=VMEM_LIMIT),
        name="merge",
    )(x, g, ya, yb, yc, yd, wg, gb, wout, wo)


def _ffn_kernel(x_ref, g_ref, wup_ref, cw_ref, wdn_ref, fg_ref, o_ref, carry_ref, *,
                final_norm):
    tm = x_ref.shape[1]

    @pl.when(pl.program_id(1) == 0)
    def _():
        carry_ref[...] = jnp.zeros_like(carry_ref)

    x = x_ref[0]
    h = _rms_norm(x, g_ref[...]).astype(BF16)
    acc = jnp.zeros((tm, D_MODEL), F32)
    for c in range(D_FF // FF_CHUNK):
        halves = []
        for off in (c * FF_CHUNK, D_FF + c * FF_CHUNK):
            cols = slice(off, off + FF_CHUNK)
            up = _dot(h, wup_ref[:, cols])
            halves.append(_causal_conv3(up, carry_ref[:, cols], cw_ref[:, cols]))
            carry_ref[:, cols] = up[tm - 8:tm]
        gate, val = halves
        act = (gate * jax.nn.sigmoid(gate) * val).astype(BF16)
        acc = acc + _dot(act, wdn_ref[c * FF_CHUNK:(c + 1) * FF_CHUNK, :])
    out = x + acc
    if final_norm:
        out = _rms_norm(out, fg_ref[...])
    o_ref[0] = out


def _ffn_call(x, g, wup, cw, wdn, fg, final_norm):
    bsz, t, d = x.shape
    tm = min(ROW_TILE, t)
    const2 = lambda b, i: (0, 0)
    tok = pl.BlockSpec((1, tm, d), lambda b, i: (b, i, 0))
    return pl.pallas_call(
        functools.partial(_ffn_kernel, final_norm=final_norm),
        grid=(bsz, t // tm),
        in_specs=[
            tok, pl.BlockSpec((1, d), const2),
            pl.BlockSpec(wup.shape, const2), pl.BlockSpec(cw.shape, const2),
            pl.BlockSpec(wdn.shape, const2), pl.BlockSpec((1, d), const2),
        ],
        out_specs=tok,
        out_shape=jax.ShapeDtypeStruct((bsz, t, d), F32),
        scratch_shapes=[pltpu.VMEM((8, 2 * D_FF), F32)],
        compiler_params=pltpu.CompilerParams(
            dimension_semantics=("arbitrary", "arbitrary"), vmem_limit_bytes=VMEM_LIMIT),
        name="ffn",
    )(x, g, wup, cw, wdn, fg)


def _layer(x, final_g, last, mix_norm_g, w_in, gate_b, sc_conv_w, sc_out, sb_out, rw_mu, rw_w0,
           rw_w2, rw_a0, rw_a2, rw_g2, rw_k_k, rw_k_a, rw_r_k, rw_gn_g, rw_gn_b, rw_out,
           sg_ln_g, sg_ln_b, sg_w, sg_b, sg_out, w_o, ffn_norm_g, w_up, ffn_conv_w, w_down):
    bsz, t, d = x.shape
    row = lambda a: a.reshape(1, -1)
    w_in_b = w_in.astype(BF16)
    sg_bias = jnp.repeat(sg_b.T, HEAD_DIM, axis=1)
    ya, yd, q, kt, v, p_rw = _inproj_call(
        x, row(mix_norm_g), w_in_b[:, :GATE_OFF], sc_conv_w, row(sg_ln_g), row(sg_ln_b),
        sg_w, sg_bias)
    yb = _sb_call(q, kt, v)
    zeros = jnp.zeros((64, BRANCH), F32)
    w2p = jnp.concatenate([rw_w2, zeros], axis=0).astype(BF16)
    a2p = jnp.concatenate([zeros, rw_a2], axis=0).astype(BF16)
    yc = _rwkv_call(p_rw, row(rw_mu), row(rw_w0), w2p, row(rw_a0), a2p, rw_g2.astype(BF16),
                    row(rw_k_k), row(rw_k_a), row(rw_r_k), row(rw_gn_g), row(rw_gn_b))
    wout = jnp.stack([sc_out, sb_out, rw_out, sg_out]).astype(BF16)
    flat = lambda a: a.reshape(bsz * t, a.shape[-1])
    x1 = _merge_call(flat(x), row(mix_norm_g), flat(ya), flat(yb), flat(yc), flat(yd),
                     w_in_b[:, GATE_OFF:], row(gate_b), wout, w_o.astype(BF16))
    return _ffn_call(x1.reshape(bsz, t, d), row(ffn_norm_g), w_up.astype(BF16), ffn_conv_w,
                     w_down.astype(BF16), row(final_g), last)


def kernel(x, mix_norm_g, w_in, gate_b, sc_conv_w, sc_out, sb_out, rw_mu, rw_w0, rw_w2, rw_a0,
           rw_a2, rw_g2, rw_k_k, rw_k_a, rw_r_k, rw_gn_g, rw_gn_b, rw_out, sg_ln_g, sg_ln_b,
           sg_w, sg_b, sg_out, w_o, ffn_norm_g, w_up, ffn_conv_w, w_down, final_norm_g):
    per_layer = (mix_norm_g, w_in, gate_b, sc_conv_w, sc_out, sb_out, rw_mu, rw_w0, rw_w2, rw_a0,
                 rw_a2, rw_g2, rw_k_k, rw_k_a, rw_r_k, rw_gn_g, rw_gn_b, rw_out, sg_ln_g,
                 sg_ln_b, sg_w, sg_b, sg_out, w_o, ffn_norm_g, w_up, ffn_conv_w, w_down)
    depth = w_in.shape[0]
    for l in range(depth):
        x = _layer(x, final_norm_g, l == depth - 1, *(p[l] for p in per_layer))
    return x
```

```python
import functools
import math

import jax
import jax.numpy as jnp
from jax import lax
from jax.experimental import pallas as pl
from jax.experimental.pallas import tpu as pltpu

F32 = jnp.float32
BF16 = jnp.bfloat16

D_MODEL = 1024
HEAD_DIM = 64
N_HEADS = 4
BRANCH = 256
D_FF = 2816
RMS_EPS = 1e-6
LN_EPS = 1e-5
RW_GN_EPS = 64e-5
RW_DECAY_SCALE = math.exp(-0.5)
SG_CHUNK = 128

SC_OFF, SB_OFF, RW_OFF, SG_OFF, GATE_OFF = 0, 768, 1536, 2560, 3072

ROW_TILE = 512
SB_BLOCK = 128
RW_TILE = 256
RW_CHUNK = 64
FF_CHUNK = 256
FFN_EXT_SLOTS = 4
SB_LOG_CUTOFF = -104.0
VMEM_LIMIT = 52 * 1024 * 1024


def _dot(a, b):
    return jnp.dot(a, b, preferred_element_type=F32)


def _dot_split_lhs(x, m, passes=3):
    acc = None
    rem = x
    for _ in range(passes):
        hi = rem.astype(BF16)
        d = _dot(hi, m)
        acc = d if acc is None else acc + d
        rem = rem - hi.astype(F32)
    return acc


def _dot_split_rhs(m, x, passes=3):
    acc = None
    rem = x
    for _ in range(passes):
        hi = rem.astype(BF16)
        d = _dot(m, hi)
        acc = d if acc is None else acc + d
        rem = rem - hi.astype(F32)
    return acc


def _rms_norm(x, g):
    return x * lax.rsqrt(jnp.mean(x * x, axis=-1, keepdims=True) + RMS_EPS) * g


def _shift_rows(u, prev8, s):
    r = pltpu.roll(u, s, 0)
    c = pltpu.roll(prev8, s, 0)
    row = lax.broadcasted_iota(jnp.int32, c.shape, 0)
    top = jnp.where(row < s, c, r[0:8])
    return jnp.concatenate([top, r[8:]], axis=0)


def _causal_conv3(ext_ref, u, prev8, w):
    tm = u.shape[0]
    ext_ref[0:8, :] = prev8
    ext_ref[8:8 + tm, :] = u
    return ext_ref[6:6 + tm, :] * w[0:1] + ext_ref[7:7 + tm, :] * w[1:2] + u * w[2:3]


def _gelu_tanh(x):
    return 0.5 * x * (1.0 + jnp.tanh(math.sqrt(2.0 / math.pi) * (x + 0.044715 * x * x * x)))


def _head_of_lane(shape, axis):
    return lax.broadcasted_iota(jnp.int32, shape, axis) // HEAD_DIM


def _inproj_kernel(x_ref, g_ref, w_ref, cw_ref, lng_ref, lnb_ref, sgw_ref, sgb_ref,
                   ya_ref, yd_ref, q_ref, kt_ref, v_ref, rw_ref, carry_ref, ext_ref):
    tm = x_ref.shape[1]

    @pl.when(pl.program_id(1) == 0)
    def _():
        carry_ref[...] = jnp.zeros_like(carry_ref)

    h = _rms_norm(x_ref[0], g_ref[...]).astype(BF16)

    p = _dot(h, w_ref[:, SC_OFF:SC_OFF + 3 * BRANCH])
    u = p[:, BRANCH:2 * BRANCH] * p[:, 2 * BRANCH:3 * BRANCH]
    conv = _causal_conv3(ext_ref, u, carry_ref[...], cw_ref[...])
    carry_ref[...] = u[tm - 8:tm]
    ya_ref[0] = (p[:, 0:BRANCH] * conv).astype(BF16)

    q = _dot(h, w_ref[:, SB_OFF:SB_OFF + BRANCH]) * (1.0 / math.sqrt(HEAD_DIM))
    q_ref[0] = q.astype(BF16)
    k = _dot(h, w_ref[:, SB_OFF + BRANCH:SB_OFF + 2 * BRANCH])
    for c in range(tm // SB_BLOCK):
        kt_ref[0, c] = k[c * SB_BLOCK:(c + 1) * SB_BLOCK].T.astype(BF16)
    v_ref[0] = _dot(h, w_ref[:, SB_OFF + 2 * BRANCH:SB_OFF + 3 * BRANCH]).astype(BF16)

    rw_ref[0] = _dot(h, w_ref[:, RW_OFF:SG_OFF])

    z = _gelu_tanh(_dot(h, w_ref[:, SG_OFF:GATE_OFF]))
    u_g = z[:, 0:BRANCH]
    vv = z[:, BRANCH:2 * BRANCH]
    mu = jnp.mean(vv, axis=-1, keepdims=True)
    vc = vv - mu
    var = jnp.mean(vc * vc, axis=-1, keepdims=True)
    vn = (vc * lax.rsqrt(var + LN_EPS) * lng_ref[...] + lnb_ref[...]).astype(BF16)
    tri = (lax.broadcasted_iota(jnp.int32, (SG_CHUNK, SG_CHUNK), 1)
           <= lax.broadcasted_iota(jnp.int32, (SG_CHUNK, SG_CHUNK), 0))
    grp = _head_of_lane((1, BRANCH), 1)
    ws = [jnp.where(tri, sgw_ref[g], 0.0).astype(BF16) for g in range(N_HEADS)]
    for c in range(tm // SG_CHUNK):
        rows = slice(c * SG_CHUNK, (c + 1) * SG_CHUNK)
        vch = vn[rows]
        mixed = sgb_ref[...]
        for g in range(N_HEADS):
            mixed = mixed + jnp.where(grp == g, _dot(ws[g], vch), 0.0)
        yd_ref[0, rows, :] = (u_g[rows] * mixed).astype(BF16)


def _inproj_call(x, g, w, cw, lng, lnb, sgw, sgb):
    bsz, t, d = x.shape
    tm = min(ROW_TILE, t)
    nkb = t // SB_BLOCK
    const2 = lambda b, i: (0, 0)
    tok = lambda width: pl.BlockSpec((1, tm, width), lambda b, i: (b, i, 0))
    return pl.pallas_call(
        _inproj_kernel,
        grid=(bsz, t // tm),
        in_specs=[
            tok(d),
            pl.BlockSpec((1, d), const2),
            pl.BlockSpec(w.shape, const2),
            pl.BlockSpec(cw.shape, const2),
            pl.BlockSpec((1, BRANCH), const2),
            pl.BlockSpec((1, BRANCH), const2),
            pl.BlockSpec(sgw.shape, lambda b, i: (0, 0, 0)),
            pl.BlockSpec(sgb.shape, const2),
        ],
        out_specs=[
            tok(BRANCH), tok(BRANCH), tok(BRANCH),
            pl.BlockSpec((1, tm // SB_BLOCK, BRANCH, SB_BLOCK), lambda b, i: (b, i, 0, 0)),
            tok(BRANCH), tok(4 * BRANCH),
        ],
        out_shape=[
            jax.ShapeDtypeStruct((bsz, t, BRANCH), BF16),
            jax.ShapeDtypeStruct((bsz, t, BRANCH), BF16),
            jax.ShapeDtypeStruct((bsz, t, BRANCH), BF16),
            jax.ShapeDtypeStruct((bsz, nkb, BRANCH, SB_BLOCK), BF16),
            jax.ShapeDtypeStruct((bsz, t, BRANCH), BF16),
            jax.ShapeDtypeStruct((bsz, t, 4 * BRANCH), F32),
        ],
        scratch_shapes=[pltpu.VMEM((8, BRANCH), F32), pltpu.VMEM((tm + 8, BRANCH), F32)],
        compiler_params=pltpu.CompilerParams(
            dimension_semantics=("arbitrary", "arbitrary"), vmem_limit_bytes=VMEM_LIMIT),
        name="inproj",
    )(x, g, w, cw, lng, lnb, sgw, sgb)


def _sb_kernel(q_ref, kt_ref, v_ref, o_ref):
    bq = q_ref.shape[1]
    m = N_HEADS * bq
    i = pl.program_id(1)
    q = q_ref[0]
    lane_head = _head_of_lane((1, BRANCH), 1)
    heads = [lane_head == h for h in range(N_HEADS)]
    q4 = jnp.concatenate([jnp.where(hd, q, jnp.zeros_like(q)) for hd in heads], axis=0)
    row = lax.broadcasted_iota(jnp.int32, (m, SB_BLOCK), 0) % bq
    col = lax.broadcasted_iota(jnp.int32, (m, SB_BLOCK), 1)
    below = col < row
    r_i = lax.broadcasted_iota(jnp.int32, (2 * SB_BLOCK, 2 * SB_BLOCK), 0) % SB_BLOCK
    c_i = lax.broadcasted_iota(jnp.int32, (2 * SB_BLOCK, 2 * SB_BLOCK), 1)
    later_all = jnp.logical_or(r_i > c_i, c_i >= SB_BLOCK).astype(BF16)

    def block(j, used, acc, diagonal):
        z = _dot(q4, kt_ref[0, j])
        l = jnp.log(1.0 + jnp.exp(-jnp.abs(z)))
        sp = jnp.maximum(z, 0.0) + l
        log_beta = jnp.minimum(z, 0.0) - l
        if diagonal:
            sp = jnp.where(below, sp, 0.0)
        hi = sp.astype(BF16)
        lo = (sp - hi.astype(F32)).astype(BF16)
        sums = _dot(jnp.concatenate([hi, lo], axis=1), later_all)
        attn = jnp.exp(log_beta - sums[:, :SB_BLOCK] - used)
        if diagonal:
            attn = jnp.where(below, attn, 0.0)
        attn = attn.astype(BF16)
        start = pl.multiple_of(j * SB_BLOCK, SB_BLOCK)
        vj = v_ref[0, pl.ds(start, SB_BLOCK), :]
        attn_wide = jnp.concatenate([attn[h * bq:(h + 1) * bq] for h in range(N_HEADS)], axis=1)
        v4 = jnp.concatenate([jnp.where(hd, vj, jnp.zeros_like(vj)) for hd in heads], axis=0)
        return used + sums[:, SB_BLOCK:], acc + _dot(attn_wide, v4)

    zero = (jnp.zeros((m, SB_BLOCK), F32), jnp.zeros((bq, BRANCH), F32))

    def diagonal_and_two():
        used, acc = block(i, *zero, True)
        used, acc = block(i - 1, used, acc, False)
        used, acc = block(i - 2, used, acc, False)
        return i - 3, used, acc

    def diagonal_only():
        return (i - 1,) + block(i, *zero, True)

    def cond(state):
        j, used, _ = state
        return jnp.logical_and(j >= 0, jnp.min(used) < -SB_LOG_CUTOFF)

    def body(state):
        j, used, acc = state
        used, acc = block(j, used, acc, False)
        return j - 1, used, acc

    _, _, acc = lax.while_loop(cond, body, lax.cond(i >= 2, diagonal_and_two, diagonal_only))
    o_ref[0] = acc.astype(BF16)


def _sb_call(q, kt, v):
    bsz, t, _ = q.shape
    bq = SB_BLOCK
    return pl.pallas_call(
        _sb_kernel,
        grid=(bsz, t // bq),
        in_specs=[
            pl.BlockSpec((1, bq, BRANCH), lambda b, i: (b, i, 0)),
            pl.BlockSpec((1,) + kt.shape[1:], lambda b, i: (b, 0, 0, 0)),
            pl.BlockSpec((1, t, BRANCH), lambda b, i: (b, 0, 0)),
        ],
        out_specs=pl.BlockSpec((1, bq, BRANCH), lambda b, i: (b, i, 0)),
        out_shape=jax.ShapeDtypeStruct((bsz, t, BRANCH), BF16),
        compiler_params=pltpu.CompilerParams(
            dimension_semantics=("arbitrary", "arbitrary"), vmem_limit_bytes=VMEM_LIMIT),
        name="stickbreak",
    )(q, kt, v)


def _rwkv_chunk(r, lw, k2, v, av, bv, state, decay_col, masks):
    head_masks, strict_low, incl_low, same_head, eye, tril_rows = masks
    c = RW_CHUNK
    tile4 = lambda a: jnp.concatenate([a] * N_HEADS, axis=0)
    by_head = lambda a: jnp.concatenate([jnp.where(m, a, 0.0) for m in head_masks], axis=0)
    pick = lambda a4: sum(jnp.where(m, a4[h * c:(h + 1) * c], 0.0)
                          for h, m in enumerate(head_masks))

    cum = _dot_split_rhs(tril_rows, lw)
    total = cum[c - 1:c]
    e_in = jnp.exp(cum)
    r_in = r * e_in
    a_in = av * e_in * jnp.exp(-lw)
    e_end = jnp.exp(-total)
    e_out = jnp.exp(total - cum)
    b_out = bv * e_out
    k_out = k2 * e_out

    lhs = jnp.concatenate([by_head(a_in * e_end), by_head(r_in * e_end)], axis=0).astype(BF16)
    bt4 = tile4(b_out).T.astype(BF16)
    kt4 = tile4(k_out).T.astype(BF16)
    gram_b = _dot(lhs, bt4)
    gram_k = _dot(lhs, kt4)
    n = N_HEADS * c
    a_ab = jnp.where(strict_low, gram_b[:n], 0.0)
    m_rb = jnp.where(incl_low, gram_b[n:], 0.0)
    a_ak = jnp.where(strict_low, gram_k[:n], 0.0)
    m_rk = jnp.where(incl_low, gram_k[n:], 0.0)

    inv = eye + a_ab
    power = a_ab
    for _ in range(int(math.log2(c)) - 1):
        pb = power.astype(BF16)
        power = _dot(pb, pb)
        inv = inv + _dot(inv.astype(BF16), power.astype(BF16))

    v4 = tile4(v).astype(BF16)
    akv = _dot(a_ak.astype(BF16), v4)
    sb = state.astype(BF16)
    from_state = _dot(jnp.concatenate([a_in, r_in], axis=0).astype(BF16), sb)
    u4 = _dot(inv.astype(BF16), (tile4(from_state[:c]) + akv).astype(BF16))
    u = pick(u4)
    y4 = _dot(m_rb.astype(BF16), tile4(u).astype(BF16)) + _dot(m_rk.astype(BF16), v4)
    y = from_state[c:] + pick(y4)
    upd = _dot(bt4, by_head(u).astype(BF16)) + _dot(kt4, by_head(v).astype(BF16))
    new_state = state * decay_col + jnp.where(same_head, upd, 0.0)
    return y, new_state


def _rwkv_kernel(p_ref, mu_ref, w0_ref, w2_ref, a0_ref, a2_ref, g2_ref, kk_ref, ka_ref,
                 rk_ref, gng_ref, gnb_ref, o_ref, prev_ref, state_ref):
    tt = p_ref.shape[1]
    c = RW_CHUNK
    n = N_HEADS * c

    @pl.when(pl.program_id(1) == 0)
    def _():
        prev_ref[...] = jnp.zeros_like(prev_ref)
        state_ref[...] = jnp.zeros_like(state_ref)

    z = p_ref[0]
    zs = z + (_shift_rows(z, prev_ref[...], 1) - z) * mu_ref[...]
    prev_ref[...] = z[tt - 8:tt]
    r = zs[:, 0:BRANCH]
    k = zs[:, BRANCH:2 * BRANCH]
    v = zs[:, 2 * BRANCH:3 * BRANCH]
    xwa = zs[:, 3 * BRANCH:3 * BRANCH + 128]
    xg = zs[:, 3 * BRANCH + 128:4 * BRANCH]
    lw = -RW_DECAY_SCALE * jax.nn.sigmoid(
        w0_ref[...] + _dot(jnp.tanh(xwa).astype(BF16), w2_ref[...]))
    a = jax.nn.sigmoid(a0_ref[...] + _dot(xwa.astype(BF16), a2_ref[...]))
    g = _dot(jax.nn.sigmoid(xg).astype(BF16), g2_ref[...])

    r_i = lax.broadcasted_iota(jnp.int32, (n, n), 0)
    c_i = lax.broadcasted_iota(jnp.int32, (n, n), 1)
    same_head = (r_i // HEAD_DIM) == (c_i // HEAD_DIM)
    ones_head = same_head.astype(BF16)
    head_sum = lambda a_: _dot_split_lhs(a_, ones_head)

    kk = k * kk_ref[...]
    kk = kk / jnp.maximum(jnp.sqrt(head_sum(kk * kk)), 1e-12)
    k2 = k * (1.0 + (a - 1.0) * ka_ref[...])
    av = -kk
    bv = kk * a

    lane_head = _head_of_lane((1, BRANCH), 1)
    masks = (
        [lane_head == h for h in range(N_HEADS)],
        jnp.logical_and(same_head, c_i < r_i),
        jnp.logical_and(same_head, c_i <= r_i),
        same_head,
        (r_i == c_i).astype(F32),
        (lax.broadcasted_iota(jnp.int32, (c, c), 1)
         <= lax.broadcasted_iota(jnp.int32, (c, c), 0)).astype(BF16),
    )

    sel = (lax.broadcasted_iota(jnp.int32, (128, tt), 1) // c
           == lax.broadcasted_iota(jnp.int32, (128, tt), 0)).astype(BF16)
    decay_cols = jnp.exp(_dot_split_rhs(sel, lw).T)

    state = state_ref[...]
    ys = []
    for j in range(tt // c):
        rows = slice(j * c, (j + 1) * c)
        y, state = _rwkv_chunk(r[rows], lw[rows], k2[rows], v[rows], av[rows], bv[rows],
                               state, decay_cols[:, j:j + 1], masks)
        ys.append(y)
    state_ref[...] = state
    y = jnp.concatenate(ys, axis=0)

    inv_n = 1.0 / HEAD_DIM
    yc = y - head_sum(y) * inv_n
    var = head_sum(yc * yc) * inv_n
    yn = yc * lax.rsqrt(var + RW_GN_EPS) * gng_ref[...] + gnb_ref[...]
    bonus = head_sum(r * k2 * rk_ref[...]) * v
    o_ref[0] = ((yn + bonus) * g).astype(BF16)


def _rwkv_call(p, mu, w0, w2p, a0, a2p, g2, k_k, k_a, r_k, gn_g, gn_b):
    bsz, t, width = p.shape
    tt = min(RW_TILE, t)
    const2 = lambda b, i: (0, 0)
    vec = pl.BlockSpec((1, BRANCH), const2)
    return pl.pallas_call(
        _rwkv_kernel,
        grid=(bsz, t // tt),
        in_specs=[
            pl.BlockSpec((1, tt, width), lambda b, i: (b, i, 0)),
            pl.BlockSpec((1, width), const2),
            vec, pl.BlockSpec(w2p.shape, const2),
            vec, pl.BlockSpec(a2p.shape, const2),
            pl.BlockSpec(g2.shape, const2),
            vec, vec, vec, vec, vec,
        ],
        out_specs=pl.BlockSpec((1, tt, BRANCH), lambda b, i: (b, i, 0)),
        out_shape=jax.ShapeDtypeStruct((bsz, t, BRANCH), BF16),
        scratch_shapes=[pltpu.VMEM((8, width), F32),
                        pltpu.VMEM((N_HEADS * RW_CHUNK, BRANCH), F32)],
        compiler_params=pltpu.CompilerParams(
            dimension_semantics=("arbitrary", "arbitrary"), vmem_limit_bytes=VMEM_LIMIT),
        name="rwkv7",
    )(p, mu, w0, w2p, a0, a2p, g2, k_k, k_a, r_k, gn_g, gn_b)


def _merge_kernel(x_ref, g_ref, ya_ref, yb_ref, yc_ref, yd_ref, wg_ref, gb_ref, wout_ref,
                  wo_ref, o_ref):
    x = x_ref[...]
    h = _rms_norm(x, g_ref[...]).astype(BF16)
    merged = None
    for i, y_ref in enumerate((ya_ref, yb_ref, yc_ref, yd_ref)):
        cols = slice(i * D_MODEL, (i + 1) * D_MODEL)
        gate = jax.nn.sigmoid(_dot(h, wg_ref[:, cols]) + gb_ref[:, cols])
        term = gate * _dot(y_ref[...], wout_ref[i])
        merged = term if merged is None else merged + term
    o_ref[...] = x + _dot(merged.astype(BF16), wo_ref[...])


def _merge_call(x, g, ya, yb, yc, yd, wg, gb, wout, wo):
    n, d = x.shape
    tm = min(ROW_TILE, n)
    const2 = lambda i: (0, 0)
    tok = lambda width: pl.BlockSpec((tm, width), lambda i: (i, 0))
    return pl.pallas_call(
        _merge_kernel,
        grid=(n // tm,),
        in_specs=[
            tok(d), pl.BlockSpec((1, d), const2),
            tok(BRANCH), tok(BRANCH), tok(BRANCH), tok(BRANCH),
            pl.BlockSpec(wg.shape, const2), pl.BlockSpec(gb.shape, const2),
            pl.BlockSpec(wout.shape, lambda i: (0, 0, 0)), pl.BlockSpec(wo.shape, const2),
        ],
        out_specs=tok(d),
        out_shape=jax.ShapeDtypeStruct((n, d), F32),
        compiler_params=pltpu.CompilerParams(
            dimension_semantics=("arbitrary",), vmem_limit_bytes=VMEM_LIMIT),
        name="merge",
    )(x, g, ya, yb, yc, yd, wg, gb, wout, wo)


def _ffn_kernel(x_ref, g_ref, wup_ref, cw_ref, wdn_ref, fg_ref, o_ref, carry_ref, ext_ref, act_ref, *,
                final_norm):
    tm = x_ref.shape[1]

    @pl.when(pl.program_id(1) == 0)
    def _():
        carry_ref[...] = jnp.zeros_like(carry_ref)

    x = x_ref[0]
    h = _rms_norm(x, g_ref[...]).astype(BF16)
    for c in range(D_FF // FF_CHUNK):
        halves = []
        for half, off in enumerate((c * FF_CHUNK, D_FF + c * FF_CHUNK)):
            cols = slice(off, off + FF_CHUNK)
            up = _dot(h, wup_ref[:, cols])
            ext = ext_ref.at[(2 * c + half) % FFN_EXT_SLOTS]
            halves.append(_causal_conv3(ext, up, carry_ref[:, cols], cw_ref[:, cols]))
            carry_ref[:, cols] = up[tm - 8:tm]
        gate, val = halves
        act_ref[:, c * FF_CHUNK:(c + 1) * FF_CHUNK] = (
            gate * jax.nn.sigmoid(gate) * val).astype(BF16)
    out = x + _dot(act_ref[...], wdn_ref[...])
    if final_norm:
        out = _rms_norm(out, fg_ref[...])
    o_ref[0] = out


def _ffn_call(x, g, wup, cw, wdn, fg, final_norm):
    bsz, t, d = x.shape
    tm = min(ROW_TILE, t)
    const2 = lambda b, i: (0, 0)
    tok = pl.BlockSpec((1, tm, d), lambda b, i: (b, i, 0))
    return pl.pallas_call(
        functools.partial(_ffn_kernel, final_norm=final_norm),
        grid=(bsz, t // tm),
        in_specs=[
            tok, pl.BlockSpec((1, d), const2),
            pl.BlockSpec(wup.shape, const2), pl.BlockSpec(cw.shape, const2),
            pl.BlockSpec(wdn.shape, const2), pl.BlockSpec((1, d), const2),
        ],
        out_specs=tok,
        out_shape=jax.ShapeDtypeStruct((bsz, t, d), F32),
        scratch_shapes=[pltpu.VMEM((8, 2 * D_FF), F32),
                        pltpu.VMEM((FFN_EXT_SLOTS, tm + 8, FF_CHUNK), F32),
                        pltpu.VMEM((tm, D_FF), BF16)],
        compiler_params=pltpu.CompilerParams(
            dimension_semantics=("arbitrary", "arbitrary"), vmem_limit_bytes=VMEM_LIMIT),
        name="ffn",
    )(x, g, wup, cw, wdn, fg)


def _layer(x, final_g, last, mix_norm_g, w_in, gate_b, sc_conv_w, sc_out, sb_out, rw_mu, rw_w0,
           rw_w2, rw_a0, rw_a2, rw_g2, rw_k_k, rw_k_a, rw_r_k, rw_gn_g, rw_gn_b, rw_out,
           sg_ln_g, sg_ln_b, sg_w, sg_b, sg_out, w_o, ffn_norm_g, w_up, ffn_conv_w, w_down):
    bsz, t, d = x.shape
    row = lambda a: a.reshape(1, -1)
    w_in_b = w_in.astype(BF16)
    sg_bias = jnp.repeat(sg_b.T, HEAD_DIM, axis=1)
    ya, yd, q, kt, v, p_rw = _inproj_call(
        x, row(mix_norm_g), w_in_b[:, :GATE_OFF], sc_conv_w, row(sg_ln_g), row(sg_ln_b),
        sg_w, sg_bias)
    yb = _sb_call(q, kt, v)
    zeros = jnp.zeros((64, BRANCH), F32)
    w2p = jnp.concatenate([rw_w2, zeros], axis=0).astype(BF16)
    a2p = jnp.concatenate([zeros, rw_a2], axis=0).astype(BF16)
    yc = _rwkv_call(p_rw, row(rw_mu), row(rw_w0), w2p, row(rw_a0), a2p, rw_g2.astype(BF16),
                    row(rw_k_k), row(rw_k_a), row(rw_r_k), row(rw_gn_g), row(rw_gn_b))
    wout = jnp.stack([sc_out, sb_out, rw_out, sg_out]).astype(BF16)
    flat = lambda a: a.reshape(bsz * t, a.shape[-1])
    x1 = _merge_call(flat(x), row(mix_norm_g), flat(ya), flat(yb), flat(yc), flat(yd),
                     w_in_b[:, GATE_OFF:], row(gate_b), wout, w_o.astype(BF16))
    return _ffn_call(x1.reshape(bsz, t, d), row(ffn_norm_g), w_up.astype(BF16), ffn_conv_w,
                     w_down.astype(BF16), row(final_g), last)


def kernel(x, mix_norm_g, w_in, gate_b, sc_conv_w, sc_out, sb_out, rw_mu, rw_w0, rw_w2, rw_a0,
           rw_a2, rw_g2, rw_k_k, rw_k_a, rw_r_k, rw_gn_g, rw_gn_b, rw_out, sg_ln_g, sg_ln_b,
           sg_w, sg_b, sg_out, w_o, ffn_norm_g, w_up, ffn_conv_w, w_down, final_norm_g):
    per_layer = (mix_norm_g, w_in, gate_b, sc_conv_w, sc_out, sb_out, rw_mu, rw_w0, rw_w2, rw_a0,
                 rw_a2, rw_g2, rw_k_k, rw_k_a, rw_r_k, rw_gn_g, rw_gn_b, rw_out, sg_ln_g,
                 sg_ln_b, sg_w, sg_b, sg_out, w_o, ffn_norm_g, w_up, ffn_conv_w, w_down)
    depth = w_in.shape[0]
    for l in range(depth):
        x = _layer(x, final_norm_g, l == depth - 1, *(p[l] for p in per_layer))
    return x
```

```python
import functools
import math

import jax
import jax.numpy as jnp
from jax import lax
from jax.experimental import pallas as pl
from jax.experimental.pallas import tpu as pltpu

F32 = jnp.float32
BF16 = jnp.bfloat16

D_MODEL = 1024
HEAD_DIM = 64
N_HEADS = 4
BRANCH = 256
D_FF = 2816
RMS_EPS = 1e-6
LN_EPS = 1e-5
RW_GN_EPS = 64e-5
RW_DECAY_SCALE = math.exp(-0.5)
SG_CHUNK = 128

SC_OFF, SB_OFF, RW_OFF, SG_OFF, GATE_OFF = 0, 768, 1536, 2560, 3072

ROW_TILE = 512
SB_BLOCK = 128
RW_TILE = 512
RW_CHUNK = 64
FF_CHUNK = 256
FFN_EXT_SLOTS = 4
SB_LOG_CUTOFF = -104.0
VMEM_LIMIT = 52 * 1024 * 1024


def _dot(a, b):
    return jnp.dot(a, b, preferred_element_type=F32)


def _dot_split_lhs(x, m, passes=3):
    acc = None
    rem = x
    for _ in range(passes):
        hi = rem.astype(BF16)
        d = _dot(hi, m)
        acc = d if acc is None else acc + d
        rem = rem - hi.astype(F32)
    return acc


def _dot_split_rhs(m, x, passes=3):
    acc = None
    rem = x
    for _ in range(passes):
        hi = rem.astype(BF16)
        d = _dot(m, hi)
        acc = d if acc is None else acc + d
        rem = rem - hi.astype(F32)
    return acc


def _rms_norm(x, g):
    return x * lax.rsqrt(jnp.mean(x * x, axis=-1, keepdims=True) + RMS_EPS) * g


def _shift_rows(u, prev8, s):
    r = pltpu.roll(u, s, 0)
    c = pltpu.roll(prev8, s, 0)
    row = lax.broadcasted_iota(jnp.int32, c.shape, 0)
    top = jnp.where(row < s, c, r[0:8])
    return jnp.concatenate([top, r[8:]], axis=0)


def _causal_conv3(ext_ref, u, prev8, w):
    tm = u.shape[0]
    ext_ref[0:8, :] = prev8
    ext_ref[8:8 + tm, :] = u
    return ext_ref[6:6 + tm, :] * w[0:1] + ext_ref[7:7 + tm, :] * w[1:2] + u * w[2:3]


def _gelu_tanh(x):
    return 0.5 * x * (1.0 + jnp.tanh(math.sqrt(2.0 / math.pi) * (x + 0.044715 * x * x * x)))


def _head_of_lane(shape, axis):
    return lax.broadcasted_iota(jnp.int32, shape, axis) // HEAD_DIM


def _inproj_kernel(x_ref, g_ref, w_ref, cw_ref, lng_ref, lnb_ref, sgw_ref, sgb_ref,
                   ya_ref, yd_ref, q_ref, kt_ref, v_ref, rw_ref, carry_ref, ext_ref):
    tm = x_ref.shape[1]

    @pl.when(pl.program_id(1) == 0)
    def _():
        carry_ref[...] = jnp.zeros_like(carry_ref)

    h = _rms_norm(x_ref[0], g_ref[...]).astype(BF16)

    p = _dot(h, w_ref[:, SC_OFF:SC_OFF + 3 * BRANCH])
    u = p[:, BRANCH:2 * BRANCH] * p[:, 2 * BRANCH:3 * BRANCH]
    conv = _causal_conv3(ext_ref, u, carry_ref[...], cw_ref[...])
    carry_ref[...] = u[tm - 8:tm]
    ya_ref[0] = (p[:, 0:BRANCH] * conv).astype(BF16)

    q = _dot(h, w_ref[:, SB_OFF:SB_OFF + BRANCH]) * (1.0 / math.sqrt(HEAD_DIM))
    q_ref[0] = q.astype(BF16)
    k = _dot(h, w_ref[:, SB_OFF + BRANCH:SB_OFF + 2 * BRANCH])
    for c in range(tm // SB_BLOCK):
        kt_ref[0, c] = k[c * SB_BLOCK:(c + 1) * SB_BLOCK].T.astype(BF16)
    v_ref[0] = _dot(h, w_ref[:, SB_OFF + 2 * BRANCH:SB_OFF + 3 * BRANCH]).astype(BF16)

    rw_ref[0] = _dot(h, w_ref[:, RW_OFF:SG_OFF])

    z = _gelu_tanh(_dot(h, w_ref[:, SG_OFF:GATE_OFF]))
    u_g = z[:, 0:BRANCH]
    vv = z[:, BRANCH:2 * BRANCH]
    mu = jnp.mean(vv, axis=-1, keepdims=True)
    vc = vv - mu
    var = jnp.mean(vc * vc, axis=-1, keepdims=True)
    vn = (vc * lax.rsqrt(var + LN_EPS) * lng_ref[...] + lnb_ref[...]).astype(BF16)
    tri = (lax.broadcasted_iota(jnp.int32, (SG_CHUNK, SG_CHUNK), 1)
           <= lax.broadcasted_iota(jnp.int32, (SG_CHUNK, SG_CHUNK), 0))
    grp = _head_of_lane((1, BRANCH), 1)
    ws = [jnp.where(tri, sgw_ref[g], 0.0).astype(BF16) for g in range(N_HEADS)]
    for c in range(tm // SG_CHUNK):
        rows = slice(c * SG_CHUNK, (c + 1) * SG_CHUNK)
        vch = vn[rows]
        mixed = sgb_ref[...]
        for g in range(N_HEADS):
            mixed = mixed + jnp.where(grp == g, _dot(ws[g], vch), 0.0)
        yd_ref[0, rows, :] = (u_g[rows] * mixed).astype(BF16)


def _inproj_call(x, g, w, cw, lng, lnb, sgw, sgb):
    bsz, t, d = x.shape
    tm = min(ROW_TILE, t)
    nkb = t // SB_BLOCK
    const2 = lambda b, i: (0, 0)
    tok = lambda width: pl.BlockSpec((1, tm, width), lambda b, i: (b, i, 0))
    return pl.pallas_call(
        _inproj_kernel,
        grid=(bsz, t // tm),
        in_specs=[
            tok(d),
            pl.BlockSpec((1, d), const2),
            pl.BlockSpec(w.shape, const2),
            pl.BlockSpec(cw.shape, const2),
            pl.BlockSpec((1, BRANCH), const2),
            pl.BlockSpec((1, BRANCH), const2),
            pl.BlockSpec(sgw.shape, lambda b, i: (0, 0, 0)),
            pl.BlockSpec(sgb.shape, const2),
        ],
        out_specs=[
            tok(BRANCH), tok(BRANCH), tok(BRANCH),
            pl.BlockSpec((1, tm // SB_BLOCK, BRANCH, SB_BLOCK), lambda b, i: (b, i, 0, 0)),
            tok(BRANCH), tok(4 * BRANCH),
        ],
        out_shape=[
            jax.ShapeDtypeStruct((bsz, t, BRANCH), BF16),
            jax.ShapeDtypeStruct((bsz, t, BRANCH), BF16),
            jax.ShapeDtypeStruct((bsz, t, BRANCH), BF16),
            jax.ShapeDtypeStruct((bsz, nkb, BRANCH, SB_BLOCK), BF16),
            jax.ShapeDtypeStruct((bsz, t, BRANCH), BF16),
            jax.ShapeDtypeStruct((bsz, t, 4 * BRANCH), F32),
        ],
        scratch_shapes=[pltpu.VMEM((8, BRANCH), F32), pltpu.VMEM((tm + 8, BRANCH), F32)],
        compiler_params=pltpu.CompilerParams(
            dimension_semantics=("arbitrary", "arbitrary"), vmem_limit_bytes=VMEM_LIMIT),
        name="inproj",
    )(x, g, w, cw, lng, lnb, sgw, sgb)


def _sb_kernel(q_ref, kt_ref, v_ref, o_ref):
    bq = q_ref.shape[1]
    m = N_HEADS * bq
    i = pl.program_id(1)
    q = q_ref[0]
    lane_head = _head_of_lane((1, BRANCH), 1)
    heads = [lane_head == h for h in range(N_HEADS)]
    q4 = jnp.concatenate([jnp.where(hd, q, jnp.zeros_like(q)) for hd in heads], axis=0)
    row = lax.broadcasted_iota(jnp.int32, (m, SB_BLOCK), 0) % bq
    col = lax.broadcasted_iota(jnp.int32, (m, SB_BLOCK), 1)
    below = col < row
    r_i = lax.broadcasted_iota(jnp.int32, (2 * SB_BLOCK, 2 * SB_BLOCK), 0) % SB_BLOCK
    c_i = lax.broadcasted_iota(jnp.int32, (2 * SB_BLOCK, 2 * SB_BLOCK), 1)
    later_all = jnp.logical_or(r_i > c_i, c_i >= SB_BLOCK).astype(BF16)

    def block(j, used, acc, diagonal):
        z = _dot(q4, kt_ref[0, j])
        l = jnp.log(1.0 + jnp.exp(-jnp.abs(z)))
        sp = jnp.maximum(z, 0.0) + l
        log_beta = jnp.minimum(z, 0.0) - l
        if diagonal:
            sp = jnp.where(below, sp, 0.0)
        hi = sp.astype(BF16)
        lo = (sp - hi.astype(F32)).astype(BF16)
        sums = _dot(jnp.concatenate([hi, lo], axis=1), later_all)
        attn = jnp.exp(log_beta - sums[:, :SB_BLOCK] - used)
        if diagonal:
            attn = jnp.where(below, attn, 0.0)
        attn = attn.astype(BF16)
        start = pl.multiple_of(j * SB_BLOCK, SB_BLOCK)
        vj = v_ref[0, pl.ds(start, SB_BLOCK), :]
        attn_wide = jnp.concatenate([attn[h * bq:(h + 1) * bq] for h in range(N_HEADS)], axis=1)
        v4 = jnp.concatenate([jnp.where(hd, vj, jnp.zeros_like(vj)) for hd in heads], axis=0)
        return used + sums[:, SB_BLOCK:], acc + _dot(attn_wide, v4)

    zero = (jnp.zeros((m, SB_BLOCK), F32), jnp.zeros((bq, BRANCH), F32))

    def diagonal_and_two():
        used, acc = block(i, *zero, True)
        used, acc = block(i - 1, used, acc, False)
        used, acc = block(i - 2, used, acc, False)
        return i - 3, used, acc

    def diagonal_only():
        return (i - 1,) + block(i, *zero, True)

    def cond(state):
        j, used, _ = state
        return jnp.logical_and(j >= 0, jnp.min(used) < -SB_LOG_CUTOFF)

    def body(state):
        j, used, acc = state
        used, acc = block(j, used, acc, False)
        return j - 1, used, acc

    _, _, acc = lax.while_loop(cond, body, lax.cond(i >= 2, diagonal_and_two, diagonal_only))
    o_ref[0] = acc.astype(BF16)


def _sb_call(q, kt, v):
    bsz, t, _ = q.shape
    bq = SB_BLOCK
    return pl.pallas_call(
        _sb_kernel,
        grid=(bsz, t // bq),
        in_specs=[
            pl.BlockSpec((1, bq, BRANCH), lambda b, i: (b, i, 0)),
            pl.BlockSpec((1,) + kt.shape[1:], lambda b, i: (b, 0, 0, 0)),
            pl.BlockSpec((1, t, BRANCH), lambda b, i: (b, 0, 0)),
        ],
        out_specs=pl.BlockSpec((1, bq, BRANCH), lambda b, i: (b, i, 0)),
        out_shape=jax.ShapeDtypeStruct((bsz, t, BRANCH), BF16),
        compiler_params=pltpu.CompilerParams(
            dimension_semantics=("arbitrary", "arbitrary"), vmem_limit_bytes=VMEM_LIMIT),
        name="stickbreak",
    )(q, kt, v)


def _rwkv_local(chunks, masks):
    same_head, strict_lc, incl_lc, eye_lc, tril3 = masks
    c = RW_CHUNK
    bf = lambda a: a.astype(BF16)
    tile4 = lambda a: jnp.concatenate([a] * N_HEADS, axis=0)
    expand = lambda a: jnp.where(same_head, tile4(bf(a)), jnp.zeros((), BF16))
    wide = lambda a, b: jnp.concatenate([a, b], axis=1)
    tall = lambda a, b: jnp.concatenate([a, b], axis=0)

    def split3(x):
        hi = bf(x)
        mid = bf(x - hi.astype(F32))
        lo = bf(x - hi.astype(F32) - mid.astype(F32))
        return jnp.concatenate([hi, mid, lo], axis=0)

    st = []
    for r, lw, k2, v, av, bv in chunks:
        cum = _dot(tril3, split3(lw))
        total = cum[c - 1:c]
        e_in = jnp.exp(cum)
        r_in = r * e_in
        a_in = av * e_in * jnp.exp(-lw)
        e_end = jnp.exp(-total)
        e_out = jnp.exp(total - cum)
        st.append(dict(v=v, r_in=r_in, a_in=a_in, b_out=bv * e_out, k_out=k2 * e_out,
                       lhs=bf(tall(a_in * e_end, r_in * e_end))))
    for d in st:
        d["bt4"] = bf(tile4(d["b_out"]).T)
        d["kt4"] = bf(tile4(d["k_out"]).T)
    for d in st:
        gram_b = _dot(d["lhs"], jnp.where(same_head, d["bt4"], jnp.zeros((), BF16)))
        gram_k = _dot(d["lhs"], jnp.where(same_head, d["kt4"], jnp.zeros((), BF16)))
        d["a_ab"] = jnp.where(strict_lc, gram_b[:c], 0.0)
        d["m_rb"] = bf(jnp.where(incl_lc, gram_b[c:], 0.0))
        d["a_ak"] = bf(jnp.where(strict_lc, gram_k[:c], 0.0))
        d["m_rk"] = bf(jnp.where(incl_lc, gram_k[c:], 0.0))
        d["inv"] = eye_lc + d["a_ab"]
        d["pow"] = d["a_ab"]
        d["pow_x"] = expand(d["a_ab"])
    for _ in range(int(math.log2(c)) - 1):
        for d in st:
            d["pow"] = _dot(bf(d["pow"]), d["pow_x"])
            d["pow_x"] = expand(d["pow"])
        for d in st:
            d["inv"] = d["inv"] + _dot(bf(d["inv"]), d["pow_x"])
    for d in st:
        d["v_x"] = expand(d["v"])
        d["akv"] = _dot(d["a_ak"], d["v_x"])
    out = []
    for d in st:
        inv = bf(d["inv"])
        p_x = expand(_dot(inv, expand(d["a_in"])))
        q_x = expand(_dot(inv, expand(d["akv"])))
        qv = tall(q_x, d["v_x"])
        e_mat = d["r_in"] + _dot(d["m_rb"], p_x)
        f_mat = _dot(wide(d["m_rb"], d["m_rk"]), qv)
        g_mat = jnp.where(same_head, _dot(d["bt4"], p_x), 0.0)
        j_mat = jnp.where(same_head, _dot(wide(d["bt4"], d["kt4"]), qv), 0.0)
        out.append((bf(e_mat), f_mat, bf(g_mat), j_mat))
    return out


def _rwkv_kernel(p_ref, mu_ref, w0_ref, w2_ref, a0_ref, a2_ref, g2_ref, kk_ref, ka_ref,
                 rk_ref, gng_ref, gnb_ref, o_ref, prev_ref, state_ref):
    tt = p_ref.shape[1]
    c = RW_CHUNK
    n = N_HEADS * c

    @pl.when(pl.program_id(1) == 0)
    def _():
        prev_ref[...] = jnp.zeros_like(prev_ref)
        state_ref[...] = jnp.zeros_like(state_ref)

    z = p_ref[0]
    zs = z + (_shift_rows(z, prev_ref[...], 1) - z) * mu_ref[...]
    prev_ref[...] = z[tt - 8:tt]
    r = zs[:, 0:BRANCH]
    k = zs[:, BRANCH:2 * BRANCH]
    v = zs[:, 2 * BRANCH:3 * BRANCH]
    xwa = zs[:, 3 * BRANCH:3 * BRANCH + 128]
    xg = zs[:, 3 * BRANCH + 128:4 * BRANCH]
    lw = -RW_DECAY_SCALE * jax.nn.sigmoid(
        w0_ref[...] + _dot(jnp.tanh(xwa).astype(BF16), w2_ref[...]))
    a = jax.nn.sigmoid(a0_ref[...] + _dot(xwa.astype(BF16), a2_ref[...]))
    g = _dot(jax.nn.sigmoid(xg).astype(BF16), g2_ref[...])

    r_i = lax.broadcasted_iota(jnp.int32, (n, n), 0)
    c_i = lax.broadcasted_iota(jnp.int32, (n, n), 1)
    same_head = (r_i // HEAD_DIM) == (c_i // HEAD_DIM)
    ones_head2 = jnp.concatenate([same_head.astype(BF16)] * 2, axis=0)

    def head_sum(x):
        hi = x.astype(BF16)
        lo = (x - hi.astype(F32)).astype(BF16)
        return _dot(jnp.concatenate([hi, lo], axis=1), ones_head2)

    kk = k * kk_ref[...]
    kk = kk / jnp.maximum(jnp.sqrt(head_sum(kk * kk)), 1e-12)
    k2 = k * (1.0 + (a - 1.0) * ka_ref[...])
    av = -kk
    bv = kk * a

    t_i = lax.broadcasted_iota(jnp.int32, (c, n), 0)
    s_i = lax.broadcasted_iota(jnp.int32, (c, n), 1) % c
    tril = (lax.broadcasted_iota(jnp.int32, (c, c), 1)
            <= lax.broadcasted_iota(jnp.int32, (c, c), 0)).astype(BF16)
    masks = (same_head, s_i < t_i, s_i <= t_i, (s_i == t_i).astype(F32),
             jnp.concatenate([tril] * 3, axis=1))

    sel = (lax.broadcasted_iota(jnp.int32, (128, tt), 1) // c
           == lax.broadcasted_iota(jnp.int32, (128, tt), 0)).astype(BF16)
    decay_cols = jnp.exp(_dot_split_rhs(sel, lw).T)

    chunks = []
    for j in range(tt // c):
        rows = slice(j * c, (j + 1) * c)
        chunks.append((r[rows], lw[rows], k2[rows], v[rows], av[rows], bv[rows]))
    state = state_ref[...]
    ys = []
    for j, (e_mat, f_mat, g_mat, j_mat) in enumerate(_rwkv_local(chunks, masks)):
        sb = state.astype(BF16)
        ys.append(_dot(e_mat, sb) + f_mat)
        state = state * decay_cols[:, j:j + 1] + _dot(g_mat, sb) + j_mat
    state_ref[...] = state
    y = jnp.concatenate(ys, axis=0)

    inv_n = 1.0 / HEAD_DIM
    yc = y - head_sum(y) * inv_n
    var = head_sum(yc * yc) * inv_n
    yn = yc * lax.rsqrt(var + RW_GN_EPS) * gng_ref[...] + gnb_ref[...]
    bonus = head_sum(r * k2 * rk_ref[...]) * v
    o_ref[0] = ((yn + bonus) * g).astype(BF16)


def _rwkv_call(p, mu, w0, w2p, a0, a2p, g2, k_k, k_a, r_k, gn_g, gn_b):
    bsz, t, width = p.shape
    tt = min(RW_TILE, t)
    const2 = lambda b, i: (0, 0)
    vec = pl.BlockSpec((1, BRANCH), const2)
    return pl.pallas_call(
        _rwkv_kernel,
        grid=(bsz, t // tt),
        in_specs=[
            pl.BlockSpec((1, tt, width), lambda b, i: (b, i, 0)),
            pl.BlockSpec((1, width), const2),
            vec, pl.BlockSpec(w2p.shape, const2),
            vec, pl.BlockSpec(a2p.shape, const2),
            pl.BlockSpec(g2.shape, const2),
            vec, vec, vec, vec, vec,
        ],
        out_specs=pl.BlockSpec((1, tt, BRANCH), lambda b, i: (b, i, 0)),
        out_shape=jax.ShapeDtypeStruct((bsz, t, BRANCH), BF16),
        scratch_shapes=[pltpu.VMEM((8, width), F32),
                        pltpu.VMEM((N_HEADS * RW_CHUNK, BRANCH), F32)],
        compiler_params=pltpu.CompilerParams(
            dimension_semantics=("arbitrary", "arbitrary"), vmem_limit_bytes=VMEM_LIMIT),
        name="rwkv7",
    )(p, mu, w0, w2p, a0, a2p, g2, k_k, k_a, r_k, gn_g, gn_b)


def _merge_kernel(x_ref, g_ref, ya_ref, yb_ref, yc_ref, yd_ref, wg_ref, gb_ref, wout_ref,
                  wo_ref, o_ref):
    x = x_ref[...]
    h = _rms_norm(x, g_ref[...]).astype(BF16)
    merged = None
    for i, y_ref in enumerate((ya_ref, yb_ref, yc_ref, yd_ref)):
        cols = slice(i * D_MODEL, (i + 1) * D_MODEL)
        gate = jax.nn.sigmoid(_dot(h, wg_ref[:, cols]) + gb_ref[:, cols])
        term = gate * _dot(y_ref[...], wout_ref[i])
        merged = term if merged is None else merged + term
    o_ref[...] = x + _dot(merged.astype(BF16), wo_ref[...])


def _merge_call(x, g, ya, yb, yc, yd, wg, gb, wout, wo):
    n, d = x.shape
    tm = min(ROW_TILE, n)
    const2 = lambda i: (0, 0)
    tok = lambda width: pl.BlockSpec((tm, width), lambda i: (i, 0))
    return pl.pallas_call(
        _merge_kernel,
        grid=(n // tm,),
        in_specs=[
            tok(d), pl.BlockSpec((1, d), const2),
            tok(BRANCH), tok(BRANCH), tok(BRANCH), tok(BRANCH),
            pl.BlockSpec(wg.shape, const2), pl.BlockSpec(gb.shape, const2),
            pl.BlockSpec(wout.shape, lambda i: (0, 0, 0)), pl.BlockSpec(wo.shape, const2),
        ],
        out_specs=tok(d),
        out_shape=jax.ShapeDtypeStruct((n, d), F32),
        compiler_params=pltpu.CompilerParams(
            dimension_semantics=("arbitrary",), vmem_limit_bytes=VMEM_LIMIT),
        name="merge",
    )(x, g, ya, yb, yc, yd, wg, gb, wout, wo)


def _ffn_kernel(x_ref, g_ref, wup_ref, cw_ref, wdn_ref, fg_ref, o_ref, carry_ref, ext_ref, act_ref, *,
                final_norm):
    tm = x_ref.shape[1]

    @pl.when(pl.program_id(1) == 0)
    def _():
        carry_ref[...] = jnp.zeros_like(carry_ref)

    x = x_ref[0]
    h = _rms_norm(x, g_ref[...]).astype(BF16)
    for c in range(D_FF // FF_CHUNK):
        halves = []
        for half, off in enumerate((c * FF_CHUNK, D_FF + c * FF_CHUNK)):
            cols = slice(off, off + FF_CHUNK)
            up = _dot(h, wup_ref[:, cols])
            ext = ext_ref.at[(2 * c + half) % FFN_EXT_SLOTS]
            halves.append(_causal_conv3(ext, up, carry_ref[:, cols], cw_ref[:, cols]))
            carry_ref[:, cols] = up[tm - 8:tm]
        gate, val = halves
        act_ref[:, c * FF_CHUNK:(c + 1) * FF_CHUNK] = (
            gate * jax.nn.sigmoid(gate) * val).astype(BF16)
    out = x + _dot(act_ref[...], wdn_ref[...])
    if final_norm:
        out = _rms_norm(out, fg_ref[...])
    o_ref[0] = out


def _ffn_call(x, g, wup, cw, wdn, fg, final_norm):
    bsz, t, d = x.shape
    tm = min(ROW_TILE, t)
    const2 = lambda b, i: (0, 0)
    tok = pl.BlockSpec((1, tm, d), lambda b, i: (b, i, 0))
    return pl.pallas_call(
        functools.partial(_ffn_kernel, final_norm=final_norm),
        grid=(bsz, t // tm),
        in_specs=[
            tok, pl.BlockSpec((1, d), const2),
            pl.BlockSpec(wup.shape, const2), pl.BlockSpec(cw.shape, const2),
            pl.BlockSpec(wdn.shape, const2), pl.BlockSpec((1, d), const2),
        ],
        out_specs=tok,
        out_shape=jax.ShapeDtypeStruct((bsz, t, d), F32),
        scratch_shapes=[pltpu.VMEM((8, 2 * D_FF), F32),
                        pltpu.VMEM((FFN_EXT_SLOTS, tm + 8, FF_CHUNK), F32),
                        pltpu.VMEM((tm, D_FF), BF16)],
        compiler_params=pltpu.CompilerParams(
            dimension_semantics=("arbitrary", "arbitrary"), vmem_limit_bytes=VMEM_LIMIT),
        name="ffn",
    )(x, g, wup, cw, wdn, fg)


def _layer(x, final_g, last, mix_norm_g, w_in, gate_b, sc_conv_w, sc_out, sb_out, rw_mu, rw_w0,
           rw_w2, rw_a0, rw_a2, rw_g2, rw_k_k, rw_k_a, rw_r_k, rw_gn_g, rw_gn_b, rw_out,
           sg_ln_g, sg_ln_b, sg_w, sg_b, sg_out, w_o, ffn_norm_g, w_up, ffn_conv_w, w_down):
    bsz, t, d = x.shape
    row = lambda a: a.reshape(1, -1)
    w_in_b = w_in.astype(BF16)
    sg_bias = jnp.repeat(sg_b.T, HEAD_DIM, axis=1)
    ya, yd, q, kt, v, p_rw = _inproj_call(
        x, row(mix_norm_g), w_in_b[:, :GATE_OFF], sc_conv_w, row(sg_ln_g), row(sg_ln_b),
        sg_w, sg_bias)
    yb = _sb_call(q, kt, v)
    zeros = jnp.zeros((64, BRANCH), F32)
    w2p = jnp.concatenate([rw_w2, zeros], axis=0).astype(BF16)
    a2p = jnp.concatenate([zeros, rw_a2], axis=0).astype(BF16)
    yc = _rwkv_call(p_rw, row(rw_mu), row(rw_w0), w2p, row(rw_a0), a2p, rw_g2.astype(BF16),
                    row(rw_k_k), row(rw_k_a), row(rw_r_k), row(rw_gn_g), row(rw_gn_b))
    wout = jnp.stack([sc_out, sb_out, rw_out, sg_out]).astype(BF16)
    flat = lambda a: a.reshape(bsz * t, a.shape[-1])
    x1 = _merge_call(flat(x), row(mix_norm_g), flat(ya), flat(yb), flat(yc), flat(yd),
                     w_in_b[:, GATE_OFF:], row(gate_b), wout, w_o.astype(BF16))
    return _ffn_call(x1.reshape(bsz, t, d), row(ffn_norm_g), w_up.astype(BF16), ffn_conv_w,
                     w_down.astype(BF16), row(final_g), last)


def kernel(x, mix_norm_g, w_in, gate_b, sc_conv_w, sc_out, sb_out, rw_mu, rw_w0, rw_w2, rw_a0,
           rw_a2, rw_g2, rw_k_k, rw_k_a, rw_r_k, rw_gn_g, rw_gn_b, rw_out, sg_ln_g, sg_ln_b,
           sg_w, sg_b, sg_out, w_o, ffn_norm_g, w_up, ffn_conv_w, w_down, final_norm_g):
    per_layer = (mix_norm_g, w_in, gate_b, sc_conv_w, sc_out, sb_out, rw_mu, rw_w0, rw_w2, rw_a0,
                 rw_a2, rw_g2, rw_k_k, rw_k_a, rw_r_k, rw_gn_g, rw_gn_b, rw_out, sg_ln_g,
                 sg_ln_b, sg_w, sg_b, sg_out, w_o, ffn_norm_g, w_up, ffn_conv_w, w_down)
    depth = w_in.shape[0]
    for l in range(depth):
        x = _layer(x, final_norm_g, l == depth - 1, *(p[l] for p in per_layer))
    return x
```

```python
import functools
import math

import jax
import jax.numpy as jnp
from jax import lax
from jax.experimental import pallas as pl
from jax.experimental.pallas import tpu as pltpu

F32 = jnp.float32
BF16 = jnp.bfloat16

D_MODEL = 1024
HEAD_DIM = 64
N_HEADS = 4
BRANCH = 256
D_FF = 2816
RMS_EPS = 1e-6
LN_EPS = 1e-5
RW_GN_EPS = 64e-5
RW_DECAY_SCALE = math.exp(-0.5)
SG_CHUNK = 128

SC_OFF, SB_OFF, RW_OFF, SG_OFF, GATE_OFF = 0, 768, 1536, 2560, 3072

ROW_TILE = 1024
SB_BLOCK = 128
RW_TILE = 512
RW_CHUNK = 64
FF_CHUNK = 256
FFN_EXT_SLOTS = 4
SB_LOG_CUTOFF = -104.0
VMEM_LIMIT = 52 * 1024 * 1024


def _dot(a, b):
    return jnp.dot(a, b, preferred_element_type=F32)


def _dot_split_lhs(x, m, passes=3):
    acc = None
    rem = x
    for _ in range(passes):
        hi = rem.astype(BF16)
        d = _dot(hi, m)
        acc = d if acc is None else acc + d
        rem = rem - hi.astype(F32)
    return acc


def _dot_split_rhs(m, x, passes=3):
    acc = None
    rem = x
    for _ in range(passes):
        hi = rem.astype(BF16)
        d = _dot(m, hi)
        acc = d if acc is None else acc + d
        rem = rem - hi.astype(F32)
    return acc


def _rms_norm(x, g):
    return x * lax.rsqrt(jnp.mean(x * x, axis=-1, keepdims=True) + RMS_EPS) * g


def _shift_rows(u, prev8, s):
    r = pltpu.roll(u, s, 0)
    c = pltpu.roll(prev8, s, 0)
    row = lax.broadcasted_iota(jnp.int32, c.shape, 0)
    top = jnp.where(row < s, c, r[0:8])
    return jnp.concatenate([top, r[8:]], axis=0)


def _causal_conv3(ext_ref, u, prev8, w):
    tm = u.shape[0]
    ext_ref[0:8, :] = prev8
    ext_ref[8:8 + tm, :] = u
    return ext_ref[6:6 + tm, :] * w[0:1] + ext_ref[7:7 + tm, :] * w[1:2] + u * w[2:3]


def _gelu_tanh(x):
    return 0.5 * x * (1.0 + jnp.tanh(math.sqrt(2.0 / math.pi) * (x + 0.044715 * x * x * x)))


def _head_of_lane(shape, axis):
    return lax.broadcasted_iota(jnp.int32, shape, axis) // HEAD_DIM


def _inproj_kernel(x_ref, g_ref, w_ref, cw_ref, lng_ref, lnb_ref, sgw_ref, sgb_ref,
                   mu_ref, ya_ref, yd_ref, q_ref, kt_ref, v_ref, rw_ref, carry_ref, ext_ref,
                   zprev_ref):
    tm = x_ref.shape[1]

    @pl.when(pl.program_id(1) == 0)
    def _():
        carry_ref[...] = jnp.zeros_like(carry_ref)
        zprev_ref[...] = jnp.zeros_like(zprev_ref)

    h = _rms_norm(x_ref[0], g_ref[...]).astype(BF16)

    p = _dot(h, w_ref[:, SC_OFF:SC_OFF + 3 * BRANCH])
    u = p[:, BRANCH:2 * BRANCH] * p[:, 2 * BRANCH:3 * BRANCH]
    conv = _causal_conv3(ext_ref, u, carry_ref[...], cw_ref[...])
    carry_ref[...] = u[tm - 8:tm]
    ya_ref[0] = (p[:, 0:BRANCH] * conv).astype(BF16)

    q = _dot(h, w_ref[:, SB_OFF:SB_OFF + BRANCH]) * (1.0 / math.sqrt(HEAD_DIM))
    q_ref[0] = q.astype(BF16)
    k = _dot(h, w_ref[:, SB_OFF + BRANCH:SB_OFF + 2 * BRANCH])
    for c in range(tm // SB_BLOCK):
        kt_ref[0, c] = k[c * SB_BLOCK:(c + 1) * SB_BLOCK].T.astype(BF16)
    v_ref[0] = _dot(h, w_ref[:, SB_OFF + 2 * BRANCH:SB_OFF + 3 * BRANCH]).astype(BF16)

    z = _dot(h, w_ref[:, RW_OFF:SG_OFF])
    rw_ref[0] = z + (_shift_rows(z, zprev_ref[...], 1) - z) * mu_ref[...]
    zprev_ref[...] = z[tm - 8:tm]

    z = _gelu_tanh(_dot(h, w_ref[:, SG_OFF:GATE_OFF]))
    u_g = z[:, 0:BRANCH]
    vv = z[:, BRANCH:2 * BRANCH]
    mu = jnp.mean(vv, axis=-1, keepdims=True)
    vc = vv - mu
    var = jnp.mean(vc * vc, axis=-1, keepdims=True)
    vn = (vc * lax.rsqrt(var + LN_EPS) * lng_ref[...] + lnb_ref[...]).astype(BF16)
    tri = (lax.broadcasted_iota(jnp.int32, (SG_CHUNK, SG_CHUNK), 1)
           <= lax.broadcasted_iota(jnp.int32, (SG_CHUNK, SG_CHUNK), 0))
    grp = _head_of_lane((1, BRANCH), 1)
    ws = [jnp.where(tri, sgw_ref[g], 0.0).astype(BF16) for g in range(N_HEADS)]
    for c in range(tm // SG_CHUNK):
        rows = slice(c * SG_CHUNK, (c + 1) * SG_CHUNK)
        vch = vn[rows]
        mixed = sgb_ref[...]
        for g in range(N_HEADS):
            mixed = mixed + jnp.where(grp == g, _dot(ws[g], vch), 0.0)
        yd_ref[0, rows, :] = (u_g[rows] * mixed).astype(BF16)


def _inproj_call(x, g, w, cw, lng, lnb, sgw, sgb, mu):
    bsz, t, d = x.shape
    tm = min(ROW_TILE, t)
    nkb = t // SB_BLOCK
    const2 = lambda b, i: (0, 0)
    tok = lambda width: pl.BlockSpec((1, tm, width), lambda b, i: (b, i, 0))
    return pl.pallas_call(
        _inproj_kernel,
        grid=(bsz, t // tm),
        in_specs=[
            tok(d),
            pl.BlockSpec((1, d), const2),
            pl.BlockSpec(w.shape, const2),
            pl.BlockSpec(cw.shape, const2),
            pl.BlockSpec((1, BRANCH), const2),
            pl.BlockSpec((1, BRANCH), const2),
            pl.BlockSpec(sgw.shape, lambda b, i: (0, 0, 0)),
            pl.BlockSpec(sgb.shape, const2),
            pl.BlockSpec(mu.shape, const2),
        ],
        out_specs=[
            tok(BRANCH), tok(BRANCH), tok(BRANCH),
            pl.BlockSpec((1, tm // SB_BLOCK, BRANCH, SB_BLOCK), lambda b, i: (b, i, 0, 0)),
            tok(BRANCH), tok(4 * BRANCH),
        ],
        out_shape=[
            jax.ShapeDtypeStruct((bsz, t, BRANCH), BF16),
            jax.ShapeDtypeStruct((bsz, t, BRANCH), BF16),
            jax.ShapeDtypeStruct((bsz, t, BRANCH), BF16),
            jax.ShapeDtypeStruct((bsz, nkb, BRANCH, SB_BLOCK), BF16),
            jax.ShapeDtypeStruct((bsz, t, BRANCH), BF16),
            jax.ShapeDtypeStruct((bsz, t, 4 * BRANCH), F32),
        ],
        scratch_shapes=[pltpu.VMEM((8, BRANCH), F32), pltpu.VMEM((tm + 8, BRANCH), F32),
                        pltpu.VMEM((8, 4 * BRANCH), F32)],
        compiler_params=pltpu.CompilerParams(
            dimension_semantics=("arbitrary", "arbitrary"), vmem_limit_bytes=VMEM_LIMIT),
        name="inproj",
    )(x, g, w, cw, lng, lnb, sgw, sgb, mu)


def _sb_kernel(q_ref, kt_ref, v_ref, o_ref):
    bq = q_ref.shape[1]
    m = N_HEADS * bq
    i = pl.program_id(1)
    q = q_ref[0]
    lane_head = _head_of_lane((1, BRANCH), 1)
    heads = [lane_head == h for h in range(N_HEADS)]
    q4 = jnp.concatenate([jnp.where(hd, q, jnp.zeros_like(q)) for hd in heads], axis=0)
    row = lax.broadcasted_iota(jnp.int32, (m, SB_BLOCK), 0) % bq
    col = lax.broadcasted_iota(jnp.int32, (m, SB_BLOCK), 1)
    below = col < row
    r_i = lax.broadcasted_iota(jnp.int32, (2 * SB_BLOCK, 2 * SB_BLOCK), 0) % SB_BLOCK
    c_i = lax.broadcasted_iota(jnp.int32, (2 * SB_BLOCK, 2 * SB_BLOCK), 1)
    later_all = jnp.logical_or(r_i > c_i, c_i >= SB_BLOCK).astype(BF16)

    def block(j, used, acc, diagonal):
        z = _dot(q4, kt_ref[0, j])
        l = jnp.log(1.0 + jnp.exp(-jnp.abs(z)))
        sp = jnp.maximum(z, 0.0) + l
        log_beta = jnp.minimum(z, 0.0) - l
        if diagonal:
            sp = jnp.where(below, sp, 0.0)
        hi = sp.astype(BF16)
        lo = (sp - hi.astype(F32)).astype(BF16)
        sums = _dot(jnp.concatenate([hi, lo], axis=1), later_all)
        attn = jnp.exp(log_beta - sums[:, :SB_BLOCK] - used)
        if diagonal:
            attn = jnp.where(below, attn, 0.0)
        attn = attn.astype(BF16)
        start = pl.multiple_of(j * SB_BLOCK, SB_BLOCK)
        vj = v_ref[0, pl.ds(start, SB_BLOCK), :]
        attn_wide = jnp.concatenate([attn[h * bq:(h + 1) * bq] for h in range(N_HEADS)], axis=1)
        v4 = jnp.concatenate([jnp.where(hd, vj, jnp.zeros_like(vj)) for hd in heads], axis=0)
        return used + sums[:, SB_BLOCK:], acc + _dot(attn_wide, v4)

    zero = (jnp.zeros((m, SB_BLOCK), F32), jnp.zeros((bq, BRANCH), F32))

    def diagonal_and_two():
        used, acc = block(i, *zero, True)
        used, acc = block(i - 1, used, acc, False)
        used, acc = block(i - 2, used, acc, False)
        return i - 3, used, acc

    def diagonal_only():
        return (i - 1,) + block(i, *zero, True)

    def cond(state):
        j, used, _ = state
        return jnp.logical_and(j >= 0, jnp.min(used) < -SB_LOG_CUTOFF)

    def body(state):
        j, used, acc = state
        used, acc = block(j, used, acc, False)
        return j - 1, used, acc

    _, _, acc = lax.while_loop(cond, body, lax.cond(i >= 2, diagonal_and_two, diagonal_only))
    o_ref[0] = acc.astype(BF16)


def _sb_call(q, kt, v):
    bsz, t, _ = q.shape
    bq = SB_BLOCK
    return pl.pallas_call(
        _sb_kernel,
        grid=(bsz, t // bq),
        in_specs=[
            pl.BlockSpec((1, bq, BRANCH), lambda b, i: (b, i, 0)),
            pl.BlockSpec((1,) + kt.shape[1:], lambda b, i: (b, 0, 0, 0)),
            pl.BlockSpec((1, t, BRANCH), lambda b, i: (b, 0, 0)),
        ],
        out_specs=pl.BlockSpec((1, bq, BRANCH), lambda b, i: (b, i, 0)),
        out_shape=jax.ShapeDtypeStruct((bsz, t, BRANCH), BF16),
        compiler_params=pltpu.CompilerParams(
            dimension_semantics=("arbitrary", "arbitrary"), vmem_limit_bytes=VMEM_LIMIT),
        name="stickbreak",
    )(q, kt, v)


def _rwkv_local(chunks, masks):
    same_head, strict_lc, incl_lc, eye_lc, tril3 = masks
    c = RW_CHUNK
    bf = lambda a: a.astype(BF16)
    tile4 = lambda a: jnp.concatenate([a] * N_HEADS, axis=0)
    expand = lambda a: jnp.where(same_head, tile4(bf(a)), jnp.zeros((), BF16))
    wide = lambda a, b: jnp.concatenate([a, b], axis=1)
    tall = lambda a, b: jnp.concatenate([a, b], axis=0)

    def split3(x):
        hi = bf(x)
        mid = bf(x - hi.astype(F32))
        lo = bf(x - hi.astype(F32) - mid.astype(F32))
        return jnp.concatenate([hi, mid, lo], axis=0)

    st = []
    for r, lw, k2, v, av, bv in chunks:
        cum = _dot(tril3, split3(lw))
        total = cum[c - 1:c]
        e_in = jnp.exp(cum)
        r_in = r * e_in
        a_in = av * e_in * jnp.exp(-lw)
        e_end = jnp.exp(-total)
        e_out = jnp.exp(total - cum)
        st.append(dict(v=v, r_in=r_in, a_in=a_in, b_out=bv * e_out, k_out=k2 * e_out,
                       lhs=bf(tall(a_in * e_end, r_in * e_end))))
    for d in st:
        d["bt4"] = bf(tile4(d["b_out"]).T)
        d["kt4"] = bf(tile4(d["k_out"]).T)
    for d in st:
        gram_b = _dot(d["lhs"], jnp.where(same_head, d["bt4"], jnp.zeros((), BF16)))
        gram_k = _dot(d["lhs"], jnp.where(same_head, d["kt4"], jnp.zeros((), BF16)))
        d["a_ab"] = jnp.where(strict_lc, gram_b[:c], 0.0)
        d["m_rb"] = bf(jnp.where(incl_lc, gram_b[c:], 0.0))
        d["a_ak"] = bf(jnp.where(strict_lc, gram_k[:c], 0.0))
        d["m_rk"] = bf(jnp.where(incl_lc, gram_k[c:], 0.0))
        d["inv"] = eye_lc + d["a_ab"]
        d["pow"] = d["a_ab"]
        d["pow_x"] = expand(d["a_ab"])
    for _ in range(int(math.log2(c)) - 1):
        for d in st:
            d["pow"] = _dot(bf(d["pow"]), d["pow_x"])
            d["pow_x"] = expand(d["pow"])
        for d in st:
            d["inv"] = d["inv"] + _dot(bf(d["inv"]), d["pow_x"])
    for d in st:
        d["v_x"] = expand(d["v"])
        d["akv"] = _dot(d["a_ak"], d["v_x"])
    out = []
    for d in st:
        inv = bf(d["inv"])
        p_x = expand(_dot(inv, expand(d["a_in"])))
        q_x = expand(_dot(inv, expand(d["akv"])))
        qv = tall(q_x, d["v_x"])
        e_mat = d["r_in"] + _dot(d["m_rb"], p_x)
        f_mat = _dot(wide(d["m_rb"], d["m_rk"]), qv)
        g_mat = jnp.where(same_head, _dot(d["bt4"], p_x), 0.0)
        j_mat = jnp.where(same_head, _dot(wide(d["bt4"], d["kt4"]), qv), 0.0)
        out.append((bf(e_mat), f_mat, bf(g_mat), j_mat))
    return out


def _rwkv_kernel(p_ref, w0_ref, w2_ref, a0_ref, a2_ref, g2_ref, kk_ref, ka_ref,
                 rk_ref, gng_ref, gnb_ref, o_ref, state_ref):
    tt = p_ref.shape[1]
    c = RW_CHUNK
    n = N_HEADS * c

    @pl.when(pl.program_id(1) == 0)
    def _():
        state_ref[...] = jnp.zeros_like(state_ref)

    zs = p_ref[0]
    r = zs[:, 0:BRANCH]
    k = zs[:, BRANCH:2 * BRANCH]
    v = zs[:, 2 * BRANCH:3 * BRANCH]
    xwa = zs[:, 3 * BRANCH:3 * BRANCH + 128]
    xg = zs[:, 3 * BRANCH + 128:4 * BRANCH]
    lw = -RW_DECAY_SCALE * jax.nn.sigmoid(
        w0_ref[...] + _dot(jnp.tanh(xwa).astype(BF16), w2_ref[...]))
    a = jax.nn.sigmoid(a0_ref[...] + _dot(xwa.astype(BF16), a2_ref[...]))
    g = _dot(jax.nn.sigmoid(xg).astype(BF16), g2_ref[...])

    r_i = lax.broadcasted_iota(jnp.int32, (n, n), 0)
    c_i = lax.broadcasted_iota(jnp.int32, (n, n), 1)
    same_head = (r_i // HEAD_DIM) == (c_i // HEAD_DIM)
    ones_head2 = jnp.concatenate([same_head.astype(BF16)] * 2, axis=0)

    def head_sum(x):
        hi = x.astype(BF16)
        lo = (x - hi.astype(F32)).astype(BF16)
        return _dot(jnp.concatenate([hi, lo], axis=1), ones_head2)

    kk = k * kk_ref[...]
    kk = kk / jnp.maximum(jnp.sqrt(head_sum(kk * kk)), 1e-12)
    k2 = k * (1.0 + (a - 1.0) * ka_ref[...])
    av = -kk
    bv = kk * a

    t_i = lax.broadcasted_iota(jnp.int32, (c, n), 0)
    s_i = lax.broadcasted_iota(jnp.int32, (c, n), 1) % c
    tril = (lax.broadcasted_iota(jnp.int32, (c, c), 1)
            <= lax.broadcasted_iota(jnp.int32, (c, c), 0)).astype(BF16)
    masks = (same_head, s_i < t_i, s_i <= t_i, (s_i == t_i).astype(F32),
             jnp.concatenate([tril] * 3, axis=1))

    sel = (lax.broadcasted_iota(jnp.int32, (128, tt), 1) // c
           == lax.broadcasted_iota(jnp.int32, (128, tt), 0)).astype(BF16)
    decay_cols = jnp.exp(_dot_split_rhs(sel, lw).T)

    chunks = []
    for j in range(tt // c):
        rows = slice(j * c, (j + 1) * c)
        chunks.append((r[rows], lw[rows], k2[rows], v[rows], av[rows], bv[rows]))
    state = state_ref[...]
    ys = []
    for j, (e_mat, f_mat, g_mat, j_mat) in enumerate(_rwkv_local(chunks, masks)):
        sb = state.astype(BF16)
        ys.append(_dot(e_mat, sb) + f_mat)
        state = state * decay_cols[:, j:j + 1] + _dot(g_mat, sb) + j_mat
    state_ref[...] = state
    y = jnp.concatenate(ys, axis=0)

    inv_n = 1.0 / HEAD_DIM
    yc = y - head_sum(y) * inv_n
    var = head_sum(yc * yc) * inv_n
    yn = yc * lax.rsqrt(var + RW_GN_EPS) * gng_ref[...] + gnb_ref[...]
    bonus = head_sum(r * k2 * rk_ref[...]) * v
    o_ref[0] = ((yn + bonus) * g).astype(BF16)


def _rwkv_call(p, w0, w2p, a0, a2p, g2, k_k, k_a, r_k, gn_g, gn_b):
    bsz, t, width = p.shape
    tt = min(RW_TILE, t)
    const2 = lambda b, i: (0, 0)
    vec = pl.BlockSpec((1, BRANCH), const2)
    return pl.pallas_call(
        _rwkv_kernel,
        grid=(bsz, t // tt),
        in_specs=[
            pl.BlockSpec((1, tt, width), lambda b, i: (b, i, 0)),
            vec, pl.BlockSpec(w2p.shape, const2),
            vec, pl.BlockSpec(a2p.shape, const2),
            pl.BlockSpec(g2.shape, const2),
            vec, vec, vec, vec, vec,
        ],
        out_specs=pl.BlockSpec((1, tt, BRANCH), lambda b, i: (b, i, 0)),
        out_shape=jax.ShapeDtypeStruct((bsz, t, BRANCH), BF16),
        scratch_shapes=[pltpu.VMEM((N_HEADS * RW_CHUNK, BRANCH), F32)],
        compiler_params=pltpu.CompilerParams(
            dimension_semantics=("arbitrary", "arbitrary"), vmem_limit_bytes=VMEM_LIMIT),
        name="rwkv7",
    )(p, w0, w2p, a0, a2p, g2, k_k, k_a, r_k, gn_g, gn_b)


def _merge_kernel(x_ref, g_ref, ya_ref, yb_ref, yc_ref, yd_ref, wg_ref, gb_ref, wout_ref,
                  wo_ref, o_ref):
    x = x_ref[...]
    h = _rms_norm(x, g_ref[...]).astype(BF16)
    merged = None
    for i, y_ref in enumerate((ya_ref, yb_ref, yc_ref, yd_ref)):
        cols = slice(i * D_MODEL, (i + 1) * D_MODEL)
        gate = jax.nn.sigmoid(_dot(h, wg_ref[:, cols]) + gb_ref[:, cols])
        term = gate * _dot(y_ref[...], wout_ref[i])
        merged = term if merged is None else merged + term
    o_ref[...] = x + _dot(merged.astype(BF16), wo_ref[...])


def _merge_call(x, g, ya, yb, yc, yd, wg, gb, wout, wo):
    n, d = x.shape
    tm = min(ROW_TILE, n)
    const2 = lambda i: (0, 0)
    tok = lambda width: pl.BlockSpec((tm, width), lambda i: (i, 0))
    return pl.pallas_call(
        _merge_kernel,
        grid=(n // tm,),
        in_specs=[
            tok(d), pl.BlockSpec((1, d), const2),
            tok(BRANCH), tok(BRANCH), tok(BRANCH), tok(BRANCH),
            pl.BlockSpec(wg.shape, const2), pl.BlockSpec(gb.shape, const2),
            pl.BlockSpec(wout.shape, lambda i: (0, 0, 0)), pl.BlockSpec(wo.shape, const2),
        ],
        out_specs=tok(d),
        out_shape=jax.ShapeDtypeStruct((n, d), F32),
        compiler_params=pltpu.CompilerParams(
            dimension_semantics=("arbitrary",), vmem_limit_bytes=VMEM_LIMIT),
        name="merge",
    )(x, g, ya, yb, yc, yd, wg, gb, wout, wo)


def _ffn_kernel(x_ref, g_ref, wup_ref, cw_ref, wdn_ref, fg_ref, o_ref, carry_ref, ext_ref, act_ref, *,
                final_norm):
    tm = x_ref.shape[1]

    @pl.when(pl.program_id(1) == 0)
    def _():
        carry_ref[...] = jnp.zeros_like(carry_ref)

    x = x_ref[0]
    h = _rms_norm(x, g_ref[...]).astype(BF16)
    for c in range(D_FF // FF_CHUNK):
        halves = []
        for half, off in enumerate((c * FF_CHUNK, D_FF + c * FF_CHUNK)):
            cols = slice(off, off + FF_CHUNK)
            up = _dot(h, wup_ref[:, cols])
            ext = ext_ref.at[(2 * c + half) % FFN_EXT_SLOTS]
            halves.append(_causal_conv3(ext, up, carry_ref[:, cols], cw_ref[:, cols]))
            carry_ref[:, cols] = up[tm - 8:tm]
        gate, val = halves
        act_ref[:, c * FF_CHUNK:(c + 1) * FF_CHUNK] = (
            gate * jax.nn.sigmoid(gate) * val).astype(BF16)
    out = x + _dot(act_ref[...], wdn_ref[...])
    if final_norm:
        out = _rms_norm(out, fg_ref[...])
    o_ref[0] = out


def _ffn_call(x, g, wup, cw, wdn, fg, final_norm):
    bsz, t, d = x.shape
    tm = min(ROW_TILE, t)
    const2 = lambda b, i: (0, 0)
    tok = pl.BlockSpec((1, tm, d), lambda b, i: (b, i, 0))
    return pl.pallas_call(
        functools.partial(_ffn_kernel, final_norm=final_norm),
        grid=(bsz, t // tm),
        in_specs=[
            tok, pl.BlockSpec((1, d), const2),
            pl.BlockSpec(wup.shape, const2, pipeline_mode=pl.Buffered(1)),
            pl.BlockSpec(cw.shape, const2),
            pl.BlockSpec(wdn.shape, const2, pipeline_mode=pl.Buffered(1)),
            pl.BlockSpec((1, d), const2),
        ],
        out_specs=tok,
        out_shape=jax.ShapeDtypeStruct((bsz, t, d), F32),
        scratch_shapes=[pltpu.VMEM((8, 2 * D_FF), F32),
                        pltpu.VMEM((FFN_EXT_SLOTS, tm + 8, FF_CHUNK), F32),
                        pltpu.VMEM((tm, D_FF), BF16)],
        compiler_params=pltpu.CompilerParams(
            dimension_semantics=("arbitrary", "arbitrary"), vmem_limit_bytes=VMEM_LIMIT),
        name="ffn",
    )(x, g, wup, cw, wdn, fg)


def _layer(x, final_g, last, mix_norm_g, w_in, gate_b, sc_conv_w, sc_out, sb_out, rw_mu, rw_w0,
           rw_w2, rw_a0, rw_a2, rw_g2, rw_k_k, rw_k_a, rw_r_k, rw_gn_g, rw_gn_b, rw_out,
           sg_ln_g, sg_ln_b, sg_w, sg_b, sg_out, w_o, ffn_norm_g, w_up, ffn_conv_w, w_down):
    bsz, t, d = x.shape
    row = lambda a: a.reshape(1, -1)
    w_in_b = w_in.astype(BF16)
    sg_bias = jnp.repeat(sg_b.T, HEAD_DIM, axis=1)
    ya, yd, q, kt, v, p_rw = _inproj_call(
        x, row(mix_norm_g), w_in_b[:, :GATE_OFF], sc_conv_w, row(sg_ln_g), row(sg_ln_b),
        sg_w, sg_bias, row(rw_mu))
    yb = _sb_call(q, kt, v)
    zeros = jnp.zeros((64, BRANCH), F32)
    w2p = jnp.concatenate([rw_w2, zeros], axis=0).astype(BF16)
    a2p = jnp.concatenate([zeros, rw_a2], axis=0).astype(BF16)
    yc = _rwkv_call(p_rw, row(rw_w0), w2p, row(rw_a0), a2p, rw_g2.astype(BF16),
                    row(rw_k_k), row(rw_k_a), row(rw_r_k), row(rw_gn_g), row(rw_gn_b))
    wout = jnp.stack([sc_out, sb_out, rw_out, sg_out]).astype(BF16)
    flat = lambda a: a.reshape(bsz * t, a.shape[-1])
    x1 = _merge_call(flat(x), row(mix_norm_g), flat(ya), flat(yb), flat(yc), flat(yd),
                     w_in_b[:, GATE_OFF:], row(gate_b), wout, w_o.astype(BF16))
    return _ffn_call(x1.reshape(bsz, t, d), row(ffn_norm_g), w_up.astype(BF16), ffn_conv_w,
                     w_down.astype(BF16), row(final_g), last)


def kernel(x, mix_norm_g, w_in, gate_b, sc_conv_w, sc_out, sb_out, rw_mu, rw_w0, rw_w2, rw_a0,
           rw_a2, rw_g2, rw_k_k, rw_k_a, rw_r_k, rw_gn_g, rw_gn_b, rw_out, sg_ln_g, sg_ln_b,
           sg_w, sg_b, sg_out, w_o, ffn_norm_g, w_up, ffn_conv_w, w_down, final_norm_g):
    per_layer = (mix_norm_g, w_in, gate_b, sc_conv_w, sc_out, sb_out, rw_mu, rw_w0, rw_w2, rw_a0,
                 rw_a2, rw_g2, rw_k_k, rw_k_a, rw_r_k, rw_gn_g, rw_gn_b, rw_out, sg_ln_g,
                 sg_ln_b, sg_w, sg_b, sg_out, w_o, ffn_norm_g, w_up, ffn_conv_w, w_down)
    depth = w_in.shape[0]
    for l in range(depth):
        x = _layer(x, final_norm_g, l == depth - 1, *(p[l] for p in per_layer))
    return x
```

```python
import functools
import math

import jax
import jax.numpy as jnp
from jax import lax
from jax.experimental import pallas as pl
from jax.experimental.pallas import tpu as pltpu

F32 = jnp.float32
BF16 = jnp.bfloat16

D_MODEL = 1024
HEAD_DIM = 64
N_HEADS = 4
BRANCH = 256
D_FF = 2816
RMS_EPS = 1e-6
LN_EPS = 1e-5
RW_GN_EPS = 64e-5
RW_DECAY_SCALE = math.exp(-0.5)
SG_CHUNK = 128

SC_OFF, SB_OFF, RW_OFF, SG_OFF, GATE_OFF = 0, 768, 1536, 2560, 3072

ROW_TILE = 1024
SB_BLOCK = 128
RW_TILE = 512
RW_CHUNK = 64
FF_CHUNK = 256
FFN_EXT_SLOTS = 4
SB_LOG_CUTOFF = -104.0
VMEM_LIMIT = 52 * 1024 * 1024


def _dot(a, b):
    return jnp.dot(a, b, preferred_element_type=F32)


def _dot_split_rhs(m, x, passes=3):
    acc = None
    rem = x
    for _ in range(passes):
        hi = rem.astype(BF16)
        d = _dot(m, hi)
        acc = d if acc is None else acc + d
        rem = rem - hi.astype(F32)
    return acc


def _rms_norm(x, g):
    return x * lax.rsqrt(jnp.mean(x * x, axis=-1, keepdims=True) + RMS_EPS) * g


def _shift_rows(u, prev8, s):
    r = pltpu.roll(u, s, 0)
    c = pltpu.roll(prev8, s, 0)
    row = lax.broadcasted_iota(jnp.int32, c.shape, 0)
    top = jnp.where(row < s, c, r[0:8])
    return jnp.concatenate([top, r[8:]], axis=0)


def _causal_conv3(ext_ref, u, prev8, w):
    tm = u.shape[0]
    ext_ref[0:8, :] = prev8
    ext_ref[8:8 + tm, :] = u
    return ext_ref[6:6 + tm, :] * w[0:1] + ext_ref[7:7 + tm, :] * w[1:2] + u * w[2:3]


def _gelu_tanh(x):
    return 0.5 * x * (1.0 + jnp.tanh(math.sqrt(2.0 / math.pi) * (x + 0.044715 * x * x * x)))


def _head_of_lane(shape, axis):
    return lax.broadcasted_iota(jnp.int32, shape, axis) // HEAD_DIM


def _inproj_kernel(x_ref, g_ref, w_ref, cw_ref, lng_ref, lnb_ref, sgw_ref, sgb_ref,
                   mu_ref, ya_ref, yd_ref, q_ref, kt_ref, v_ref, rw_ref, carry_ref, ext_ref,
                   zprev_ref):
    tm = x_ref.shape[1]

    @pl.when(pl.program_id(1) == 0)
    def _():
        carry_ref[...] = jnp.zeros_like(carry_ref)
        zprev_ref[...] = jnp.zeros_like(zprev_ref)

    h = _rms_norm(x_ref[0], g_ref[...]).astype(BF16)

    p = _dot(h, w_ref[:, SC_OFF:SC_OFF + 3 * BRANCH])
    u = p[:, BRANCH:2 * BRANCH] * p[:, 2 * BRANCH:3 * BRANCH]
    conv = _causal_conv3(ext_ref, u, carry_ref[...], cw_ref[...])
    carry_ref[...] = u[tm - 8:tm]
    ya_ref[0] = (p[:, 0:BRANCH] * conv).astype(BF16)

    q = _dot(h, w_ref[:, SB_OFF:SB_OFF + BRANCH]) * (1.0 / math.sqrt(HEAD_DIM))
    q_ref[0] = q.astype(BF16)
    k = _dot(h, w_ref[:, SB_OFF + BRANCH:SB_OFF + 2 * BRANCH])
    for c in range(tm // SB_BLOCK):
        kt_ref[0, c] = k[c * SB_BLOCK:(c + 1) * SB_BLOCK].T.astype(BF16)
    v_ref[0] = _dot(h, w_ref[:, SB_OFF + 2 * BRANCH:SB_OFF + 3 * BRANCH]).astype(BF16)

    z = _dot(h, w_ref[:, RW_OFF:SG_OFF])
    rw_ref[0] = z + (_shift_rows(z, zprev_ref[...], 1) - z) * mu_ref[...]
    zprev_ref[...] = z[tm - 8:tm]

    z = _gelu_tanh(_dot(h, w_ref[:, SG_OFF:GATE_OFF]))
    u_g = z[:, 0:BRANCH]
    vv = z[:, BRANCH:2 * BRANCH]
    mu = jnp.mean(vv, axis=-1, keepdims=True)
    vc = vv - mu
    var = jnp.mean(vc * vc, axis=-1, keepdims=True)
    vn = (vc * lax.rsqrt(var + LN_EPS) * lng_ref[...] + lnb_ref[...]).astype(BF16)
    tri = (lax.broadcasted_iota(jnp.int32, (SG_CHUNK, SG_CHUNK), 1)
           <= lax.broadcasted_iota(jnp.int32, (SG_CHUNK, SG_CHUNK), 0))
    grp = _head_of_lane((1, BRANCH), 1)
    ws = [jnp.where(tri, sgw_ref[g], 0.0).astype(BF16) for g in range(N_HEADS)]
    for c in range(tm // SG_CHUNK):
        rows = slice(c * SG_CHUNK, (c + 1) * SG_CHUNK)
        vch = vn[rows]
        mixed = sgb_ref[...]
        for g in range(N_HEADS):
            mixed = mixed + jnp.where(grp == g, _dot(ws[g], vch), 0.0)
        yd_ref[0, rows, :] = (u_g[rows] * mixed).astype(BF16)


def _inproj_call(x, g, w, cw, lng, lnb, sgw, sgb, mu):
    bsz, t, d = x.shape
    tm = min(ROW_TILE, t)
    nkb = t // SB_BLOCK
    const2 = lambda b, i: (0, 0)
    tok = lambda width: pl.BlockSpec((1, tm, width), lambda b, i: (b, i, 0))
    return pl.pallas_call(
        _inproj_kernel,
        grid=(bsz, t // tm),
        in_specs=[
            tok(d),
            pl.BlockSpec((1, d), const2),
            pl.BlockSpec((d, GATE_OFF), const2),
            pl.BlockSpec(cw.shape, const2),
            pl.BlockSpec((1, BRANCH), const2),
            pl.BlockSpec((1, BRANCH), const2),
            pl.BlockSpec(sgw.shape, lambda b, i: (0, 0, 0)),
            pl.BlockSpec(sgb.shape, const2),
            pl.BlockSpec(mu.shape, const2),
        ],
        out_specs=[
            tok(BRANCH), tok(BRANCH), tok(BRANCH),
            pl.BlockSpec((1, tm // SB_BLOCK, BRANCH, SB_BLOCK), lambda b, i: (b, i, 0, 0)),
            tok(BRANCH), tok(4 * BRANCH),
        ],
        out_shape=[
            jax.ShapeDtypeStruct((bsz, t, BRANCH), BF16),
            jax.ShapeDtypeStruct((bsz, t, BRANCH), BF16),
            jax.ShapeDtypeStruct((bsz, t, BRANCH), BF16),
            jax.ShapeDtypeStruct((bsz, nkb, BRANCH, SB_BLOCK), BF16),
            jax.ShapeDtypeStruct((bsz, t, BRANCH), BF16),
            jax.ShapeDtypeStruct((bsz, t, 4 * BRANCH), F32),
        ],
        scratch_shapes=[pltpu.VMEM((8, BRANCH), F32), pltpu.VMEM((tm + 8, BRANCH), F32),
                        pltpu.VMEM((8, 4 * BRANCH), F32)],
        compiler_params=pltpu.CompilerParams(
            dimension_semantics=("arbitrary", "arbitrary"), vmem_limit_bytes=VMEM_LIMIT),
        name="inproj",
    )(x, g, w, cw, lng, lnb, sgw, sgb, mu)


def _sb_kernel(q_ref, kt_ref, v_ref, o_ref):
    bq = q_ref.shape[1]
    m = N_HEADS * bq
    i = pl.program_id(1)
    q = q_ref[0]
    lane_head = _head_of_lane((1, BRANCH), 1)
    heads = [lane_head == h for h in range(N_HEADS)]
    q4 = jnp.concatenate([jnp.where(hd, q, jnp.zeros_like(q)) for hd in heads], axis=0)
    row = lax.broadcasted_iota(jnp.int32, (m, SB_BLOCK), 0) % bq
    col = lax.broadcasted_iota(jnp.int32, (m, SB_BLOCK), 1)
    below = col < row
    r_i = lax.broadcasted_iota(jnp.int32, (2 * SB_BLOCK, 2 * SB_BLOCK), 0) % SB_BLOCK
    c_i = lax.broadcasted_iota(jnp.int32, (2 * SB_BLOCK, 2 * SB_BLOCK), 1)
    later_all = jnp.logical_or(r_i > c_i, c_i >= SB_BLOCK).astype(BF16)

    def block(j, used, acc, diagonal):
        z = _dot(q4, kt_ref[0, j])
        l = jnp.log(1.0 + jnp.exp(-jnp.abs(z)))
        sp = jnp.maximum(z, 0.0) + l
        log_beta = jnp.minimum(z, 0.0) - l
        if diagonal:
            sp = jnp.where(below, sp, 0.0)
        hi = sp.astype(BF16)
        lo = (sp - hi.astype(F32)).astype(BF16)
        sums = _dot(jnp.concatenate([hi, lo], axis=1), later_all)
        attn = jnp.exp(log_beta - sums[:, :SB_BLOCK] - used)
        if diagonal:
            attn = jnp.where(below, attn, 0.0)
        attn = attn.astype(BF16)
        start = pl.multiple_of(j * SB_BLOCK, SB_BLOCK)
        vj = v_ref[0, pl.ds(start, SB_BLOCK), :]
        attn_wide = jnp.concatenate([attn[h * bq:(h + 1) * bq] for h in range(N_HEADS)], axis=1)
        v4 = jnp.concatenate([jnp.where(hd, vj, jnp.zeros_like(vj)) for hd in heads], axis=0)
        return used + sums[:, SB_BLOCK:], acc + _dot(attn_wide, v4)

    zero = (jnp.zeros((m, SB_BLOCK), F32), jnp.zeros((bq, BRANCH), F32))

    def diagonal_and_two():
        used, acc = block(i, *zero, True)
        used, acc = block(i - 1, used, acc, False)
        used, acc = block(i - 2, used, acc, False)
        return i - 3, used, acc

    def diagonal_only():
        return (i - 1,) + block(i, *zero, True)

    def cond(state):
        j, used, _ = state
        return jnp.logical_and(j >= 0, jnp.min(used) < -SB_LOG_CUTOFF)

    def body(state):
        j, used, acc = state
        used, acc = block(j, used, acc, False)
        return j - 1, used, acc

    _, _, acc = lax.while_loop(cond, body, lax.cond(i >= 2, diagonal_and_two, diagonal_only))
    o_ref[0] = acc.astype(BF16)


def _sb_call(q, kt, v):
    bsz, t, _ = q.shape
    bq = SB_BLOCK
    return pl.pallas_call(
        _sb_kernel,
        grid=(bsz, t // bq),
        in_specs=[
            pl.BlockSpec((1, bq, BRANCH), lambda b, i: (b, i, 0)),
            pl.BlockSpec((1,) + kt.shape[1:], lambda b, i: (b, 0, 0, 0)),
            pl.BlockSpec((1, t, BRANCH), lambda b, i: (b, 0, 0)),
        ],
        out_specs=pl.BlockSpec((1, bq, BRANCH), lambda b, i: (b, i, 0)),
        out_shape=jax.ShapeDtypeStruct((bsz, t, BRANCH), BF16),
        compiler_params=pltpu.CompilerParams(
            dimension_semantics=("arbitrary", "arbitrary"), vmem_limit_bytes=VMEM_LIMIT),
        name="stickbreak",
    )(q, kt, v)


def _rwkv_local(chunks, masks):
    same_head, strict_lc, incl_lc, eye_lc, tril3 = masks
    c = RW_CHUNK
    bf = lambda a: a.astype(BF16)
    tile4 = lambda a: jnp.concatenate([a] * N_HEADS, axis=0)
    expand = lambda a: jnp.where(same_head, tile4(bf(a)), jnp.zeros((), BF16))
    wide = lambda a, b: jnp.concatenate([a, b], axis=1)
    tall = lambda a, b: jnp.concatenate([a, b], axis=0)

    def split3(x):
        hi = bf(x)
        mid = bf(x - hi.astype(F32))
        lo = bf(x - hi.astype(F32) - mid.astype(F32))
        return jnp.concatenate([hi, mid, lo], axis=0)

    st = []
    for r, lw, k2, v, av, bv in chunks:
        cum = _dot(tril3, split3(lw))
        total = cum[c - 1:c]
        e_in = jnp.exp(cum)
        r_in = r * e_in
        a_in = av * e_in * jnp.exp(-lw)
        e_end = jnp.exp(-total)
        e_out = jnp.exp(total - cum)
        st.append(dict(v=v, r_in=r_in, a_in=a_in, b_out=bv * e_out, k_out=k2 * e_out,
                       lhs=bf(tall(a_in * e_end, r_in * e_end))))
    for d in st:
        d["bt4"] = bf(tile4(d["b_out"]).T)
        d["kt4"] = bf(tile4(d["k_out"]).T)
    for d in st:
        gram_b = _dot(d["lhs"], jnp.where(same_head, d["bt4"], jnp.zeros((), BF16)))
        gram_k = _dot(d["lhs"], jnp.where(same_head, d["kt4"], jnp.zeros((), BF16)))
        d["a_ab"] = jnp.where(strict_lc, gram_b[:c], 0.0)
        d["m_rb"] = bf(jnp.where(incl_lc, gram_b[c:], 0.0))
        d["a_ak"] = bf(jnp.where(strict_lc, gram_k[:c], 0.0))
        d["m_rk"] = bf(jnp.where(incl_lc, gram_k[c:], 0.0))
        d["inv"] = eye_lc + d["a_ab"]
        d["pow"] = d["a_ab"]
        d["pow_x"] = expand(d["a_ab"])
    for _ in range(int(math.log2(c)) - 1):
        for d in st:
            d["pow"] = _dot(bf(d["pow"]), d["pow_x"])
            d["pow_x"] = expand(d["pow"])
        for d in st:
            d["inv"] = d["inv"] + _dot(bf(d["inv"]), d["pow_x"])
    for d in st:
        d["v_x"] = expand(d["v"])
        d["akv"] = _dot(d["a_ak"], d["v_x"])
    out = []
    for d in st:
        inv = bf(d["inv"])
        p_x = expand(_dot(inv, expand(d["a_in"])))
        q_x = expand(_dot(inv, expand(d["akv"])))
        qv = tall(q_x, d["v_x"])
        e_mat = d["r_in"] + _dot(d["m_rb"], p_x)
        f_mat = _dot(wide(d["m_rb"], d["m_rk"]), qv)
        g_mat = jnp.where(same_head, _dot(d["bt4"], p_x), 0.0)
        j_mat = jnp.where(same_head, _dot(wide(d["bt4"], d["kt4"]), qv), 0.0)
        out.append((bf(e_mat), f_mat, bf(g_mat), j_mat))
    return out


def _rwkv_kernel(p_ref, w0_ref, w2_ref, a0_ref, a2_ref, g2_ref, kk_ref, ka_ref,
                 rk_ref, gng_ref, gnb_ref, o_ref, state_ref):
    tt = p_ref.shape[1]
    c = RW_CHUNK
    n = N_HEADS * c

    @pl.when(pl.program_id(1) == 0)
    def _():
        state_ref[...] = jnp.zeros_like(state_ref)

    zs = p_ref[0]
    r = zs[:, 0:BRANCH]
    k = zs[:, BRANCH:2 * BRANCH]
    v = zs[:, 2 * BRANCH:3 * BRANCH]
    xwa = zs[:, 3 * BRANCH:3 * BRANCH + 128]
    xg = zs[:, 3 * BRANCH + 128:4 * BRANCH]
    lw = -RW_DECAY_SCALE * jax.nn.sigmoid(
        w0_ref[...] + _dot(jnp.tanh(xwa).astype(BF16), w2_ref[...]))
    a = jax.nn.sigmoid(a0_ref[...] + _dot(xwa.astype(BF16), a2_ref[...]))
    g = _dot(jax.nn.sigmoid(xg).astype(BF16), g2_ref[...])

    r_i = lax.broadcasted_iota(jnp.int32, (n, n), 0)
    c_i = lax.broadcasted_iota(jnp.int32, (n, n), 1)
    same_head = (r_i // HEAD_DIM) == (c_i // HEAD_DIM)
    ones_head2 = jnp.concatenate([same_head.astype(BF16)] * 2, axis=0)

    def head_sum(x):
        hi = x.astype(BF16)
        lo = (x - hi.astype(F32)).astype(BF16)
        return _dot(jnp.concatenate([hi, lo], axis=1), ones_head2)

    kk = k * kk_ref[...]
    kk = kk / jnp.maximum(jnp.sqrt(head_sum(kk * kk)), 1e-12)
    k2 = k * (1.0 + (a - 1.0) * ka_ref[...])
    av = -kk
    bv = kk * a

    t_i = lax.broadcasted_iota(jnp.int32, (c, n), 0)
    s_i = lax.broadcasted_iota(jnp.int32, (c, n), 1) % c
    tril = (lax.broadcasted_iota(jnp.int32, (c, c), 1)
            <= lax.broadcasted_iota(jnp.int32, (c, c), 0)).astype(BF16)
    masks = (same_head, s_i < t_i, s_i <= t_i, (s_i == t_i).astype(F32),
             jnp.concatenate([tril] * 3, axis=1))

    sel = (lax.broadcasted_iota(jnp.int32, (128, tt), 1) // c
           == lax.broadcasted_iota(jnp.int32, (128, tt), 0)).astype(BF16)
    decay_cols = jnp.exp(_dot_split_rhs(sel, lw).T)

    chunks = []
    for j in range(tt // c):
        rows = slice(j * c, (j + 1) * c)
        chunks.append((r[rows], lw[rows], k2[rows], v[rows], av[rows], bv[rows]))
    state = state_ref[...]
    ys = []
    for j, (e_mat, f_mat, g_mat, j_mat) in enumerate(_rwkv_local(chunks, masks)):
        sb = state.astype(BF16)
        ys.append(_dot(e_mat, sb) + f_mat)
        state = state * decay_cols[:, j:j + 1] + _dot(g_mat, sb) + j_mat
    state_ref[...] = state
    y = jnp.concatenate(ys, axis=0)

    inv_n = 1.0 / HEAD_DIM
    yc = y - head_sum(y) * inv_n
    var = head_sum(yc * yc) * inv_n
    yn = yc * lax.rsqrt(var + RW_GN_EPS) * gng_ref[...] + gnb_ref[...]
    bonus = head_sum(r * k2 * rk_ref[...]) * v
    o_ref[0] = ((yn + bonus) * g).astype(BF16)


def _rwkv_call(p, w0, w2p, a0, a2p, g2, k_k, k_a, r_k, gn_g, gn_b):
    bsz, t, width = p.shape
    tt = min(RW_TILE, t)
    const2 = lambda b, i: (0, 0)
    vec = pl.BlockSpec((1, BRANCH), const2)
    return pl.pallas_call(
        _rwkv_kernel,
        grid=(bsz, t // tt),
        in_specs=[
            pl.BlockSpec((1, tt, width), lambda b, i: (b, i, 0)),
            vec, pl.BlockSpec(w2p.shape, const2),
            vec, pl.BlockSpec(a2p.shape, const2),
            pl.BlockSpec(g2.shape, const2),
            vec, vec, vec, vec, vec,
        ],
        out_specs=pl.BlockSpec((1, tt, BRANCH), lambda b, i: (b, i, 0)),
        out_shape=jax.ShapeDtypeStruct((bsz, t, BRANCH), BF16),
        scratch_shapes=[pltpu.VMEM((N_HEADS * RW_CHUNK, BRANCH), F32)],
        compiler_params=pltpu.CompilerParams(
            dimension_semantics=("arbitrary", "arbitrary"), vmem_limit_bytes=VMEM_LIMIT),
        name="rwkv7",
    )(p, w0, w2p, a0, a2p, g2, k_k, k_a, r_k, gn_g, gn_b)


def _merge_kernel(x_ref, g_ref, ya_ref, yb_ref, yc_ref, yd_ref, wg0_ref, wg1_ref, wg2_ref,
                  wg3_ref, gb_ref, wout_ref, wo_ref, o_ref):
    x = x_ref[...]
    h = _rms_norm(x, g_ref[...]).astype(BF16)
    merged = None
    branches = ((ya_ref, wg0_ref), (yb_ref, wg1_ref), (yc_ref, wg2_ref), (yd_ref, wg3_ref))
    for i, (y_ref, wg_ref) in enumerate(branches):
        cols = slice(i * D_MODEL, (i + 1) * D_MODEL)
        gate = jax.nn.sigmoid(_dot(h, wg_ref[...]) + gb_ref[:, cols])
        term = gate * _dot(y_ref[...], wout_ref[i])
        merged = term if merged is None else merged + term
    o_ref[...] = x + _dot(merged.astype(BF16), wo_ref[...])


def _merge_call(x, g, ya, yb, yc, yd, w_in, gb, wout, wo):
    n, d = x.shape
    tm = min(ROW_TILE, n)
    const2 = lambda i: (0, 0)
    tok = lambda width: pl.BlockSpec((tm, width), lambda i: (i, 0))
    return pl.pallas_call(
        _merge_kernel,
        grid=(n // tm,),
        in_specs=[
            tok(d), pl.BlockSpec((1, d), const2),
            tok(BRANCH), tok(BRANCH), tok(BRANCH), tok(BRANCH),
            *[pl.BlockSpec((d, d), lambda i, j=j: (0, GATE_OFF // D_MODEL + j)) for j in range(4)],
            pl.BlockSpec(gb.shape, const2),
            pl.BlockSpec(wout.shape, lambda i: (0, 0, 0)), pl.BlockSpec(wo.shape, const2),
        ],
        out_specs=tok(d),
        out_shape=jax.ShapeDtypeStruct((n, d), F32),
        compiler_params=pltpu.CompilerParams(
            dimension_semantics=("arbitrary",), vmem_limit_bytes=VMEM_LIMIT),
        name="merge",
    )(x, g, ya, yb, yc, yd, w_in, w_in, w_in, w_in, gb, wout, wo)


def _ffn_kernel(x_ref, g_ref, wup_ref, cw_ref, wdn_ref, fg_ref, o_ref, carry_ref, ext_ref, act_ref, *,
                final_norm):
    tm = x_ref.shape[1]

    @pl.when(pl.program_id(1) == 0)
    def _():
        carry_ref[...] = jnp.zeros_like(carry_ref)

    x = x_ref[0]
    h = _rms_norm(x, g_ref[...]).astype(BF16)
    for c in range(D_FF // FF_CHUNK):
        halves = []
        for half, off in enumerate((c * FF_CHUNK, D_FF + c * FF_CHUNK)):
            cols = slice(off, off + FF_CHUNK)
            up = _dot(h, wup_ref[:, cols])
            ext = ext_ref.at[(2 * c + half) % FFN_EXT_SLOTS]
            halves.append(_causal_conv3(ext, up, carry_ref[:, cols], cw_ref[:, cols]))
            carry_ref[:, cols] = up[tm - 8:tm]
        gate, val = halves
        act_ref[:, c * FF_CHUNK:(c + 1) * FF_CHUNK] = (
            gate * jax.nn.sigmoid(gate) * val).astype(BF16)
    out = x + _dot(act_ref[...], wdn_ref[...])
    if final_norm:
        out = _rms_norm(out, fg_ref[...])
    o_ref[0] = out


def _ffn_call(x, g, wup, cw, wdn, fg, final_norm):
    bsz, t, d = x.shape
    tm = min(ROW_TILE, t)
    const2 = lambda b, i: (0, 0)
    tok = pl.BlockSpec((1, tm, d), lambda b, i: (b, i, 0))
    return pl.pallas_call(
        functools.partial(_ffn_kernel, final_norm=final_norm),
        grid=(bsz, t // tm),
        in_specs=[
            tok, pl.BlockSpec((1, d), const2),
            pl.BlockSpec(wup.shape, const2, pipeline_mode=pl.Buffered(1)),
            pl.BlockSpec(cw.shape, const2),
            pl.BlockSpec(wdn.shape, const2, pipeline_mode=pl.Buffered(1)),
            pl.BlockSpec((1, d), const2),
        ],
        out_specs=tok,
        out_shape=jax.ShapeDtypeStruct((bsz, t, d), F32),
        scratch_shapes=[pltpu.VMEM((8, 2 * D_FF), F32),
                        pltpu.VMEM((FFN_EXT_SLOTS, tm + 8, FF_CHUNK), F32),
                        pltpu.VMEM((tm, D_FF), BF16)],
        compiler_params=pltpu.CompilerParams(
            dimension_semantics=("arbitrary", "arbitrary"), vmem_limit_bytes=VMEM_LIMIT),
        name="ffn",
    )(x, g, wup, cw, wdn, fg)


def _layer(x, final_g, last, mix_norm_g, w_in, gate_b, sc_conv_w, sc_out, sb_out, rw_mu, rw_w0,
           rw_w2, rw_a0, rw_a2, rw_g2, rw_k_k, rw_k_a, rw_r_k, rw_gn_g, rw_gn_b, rw_out,
           sg_ln_g, sg_ln_b, sg_w, sg_b, sg_out, w_o, ffn_norm_g, w_up, ffn_conv_w, w_down):
    bsz, t, d = x.shape
    row = lambda a: a.reshape(1, -1)
    w_in_b = w_in.astype(BF16)
    sg_bias = jnp.repeat(sg_b.T, HEAD_DIM, axis=1)
    ya, yd, q, kt, v, p_rw = _inproj_call(
        x, row(mix_norm_g), w_in_b, sc_conv_w, row(sg_ln_g), row(sg_ln_b),
        sg_w, sg_bias, row(rw_mu))
    yb = _sb_call(q, kt, v)
    zeros = jnp.zeros((64, BRANCH), F32)
    w2p = jnp.concatenate([rw_w2, zeros], axis=0).astype(BF16)
    a2p = jnp.concatenate([zeros, rw_a2], axis=0).astype(BF16)
    yc = _rwkv_call(p_rw, row(rw_w0), w2p, row(rw_a0), a2p, rw_g2.astype(BF16),
                    row(rw_k_k), row(rw_k_a), row(rw_r_k), row(rw_gn_g), row(rw_gn_b))
    wout = jnp.stack([sc_out, sb_out, rw_out, sg_out]).astype(BF16)
    flat = lambda a: a.reshape(bsz * t, a.shape[-1])
    x1 = _merge_call(flat(x), row(mix_norm_g), flat(ya), flat(yb), flat(yc), flat(yd),
                     w_in_b, row(gate_b), wout, w_o.astype(BF16))
    return _ffn_call(x1.reshape(bsz, t, d), row(ffn_norm_g), w_up.astype(BF16), ffn_conv_w,
                     w_down.astype(BF16), row(final_g), last)


def kernel(x, mix_norm_g, w_in, gate_b, sc_conv_w, sc_out, sb_out, rw_mu, rw_w0, rw_w2, rw_a0,
           rw_a2, rw_g2, rw_k_k, rw_k_a, rw_r_k, rw_gn_g, rw_gn_b, rw_out, sg_ln_g, sg_ln_b,
           sg_w, sg_b, sg_out, w_o, ffn_norm_g, w_up, ffn_conv_w, w_down, final_norm_g):
    per_layer = (mix_norm_g, w_in, gate_b, sc_conv_w, sc_out, sb_out, rw_mu, rw_w0, rw_w2, rw_a0,
                 rw_a2, rw_g2, rw_k_k, rw_k_a, rw_r_k, rw_gn_g, rw_gn_b, rw_out, sg_ln_g,
                 sg_ln_b, sg_w, sg_b, sg_out, w_o, ffn_norm_g, w_up, ffn_conv_w, w_down)
    depth = w_in.shape[0]
    for l in range(depth):
        x = _layer(x, final_norm_g, l == depth - 1, *(p[l] for p in per_layer))
    return x
```

```python
import functools
import math

import jax
import jax.numpy as jnp
from jax import lax
from jax.experimental import pallas as pl
from jax.experimental.pallas import tpu as pltpu

F32 = jnp.float32
BF16 = jnp.bfloat16

D_MODEL = 1024
HEAD_DIM = 64
N_HEADS = 4
BRANCH = 256
D_FF = 2816
RMS_EPS = 1e-6
LN_EPS = 1e-5
RW_GN_EPS = 64e-5
RW_DECAY_SCALE = math.exp(-0.5)
SG_CHUNK = 128

SC_OFF, SB_OFF, RW_OFF, SG_OFF, GATE_OFF = 0, 768, 1536, 2560, 3072

ROW_TILE = 1024
SB_BLOCK = 128
RW_TILE = 512
RW_CHUNK = 64
FF_CHUNK = 256
SB_LOG_CUTOFF = -104.0
VMEM_LIMIT = 52 * 1024 * 1024


def _dot(a, b):
    return jnp.dot(a, b, preferred_element_type=F32)


def _dot_split_rhs(m, x, passes=3):
    acc = None
    rem = x
    for _ in range(passes):
        hi = rem.astype(BF16)
        d = _dot(m, hi)
        acc = d if acc is None else acc + d
        rem = rem - hi.astype(F32)
    return acc


def _rms_norm(x, g):
    return x * lax.rsqrt(jnp.mean(x * x, axis=-1, keepdims=True) + RMS_EPS) * g


def _shift_rows(u, prev8, s):
    r = pltpu.roll(u, s, 0)
    c = pltpu.roll(prev8, s, 0)
    row = lax.broadcasted_iota(jnp.int32, c.shape, 0)
    top = jnp.where(row < s, c, r[0:8])
    return jnp.concatenate([top, r[8:]], axis=0)


def _causal_conv3(u, prev8, w):
    return (_shift_rows(u, prev8, 2) * w[0:1] + _shift_rows(u, prev8, 1) * w[1:2]
            + u * w[2:3])


def _gelu_tanh(x):
    return 0.5 * x * (1.0 + jnp.tanh(math.sqrt(2.0 / math.pi) * (x + 0.044715 * x * x * x)))


def _head_of_lane(shape, axis):
    return lax.broadcasted_iota(jnp.int32, shape, axis) // HEAD_DIM


def _inproj_kernel(x_ref, g_ref, w_ref, cw_ref, lng_ref, lnb_ref, sgw_ref, sgb_ref,
                   mu_ref, ya_ref, yd_ref, q_ref, kt_ref, v_ref, rw_ref, carry_ref, zprev_ref):
    tm = x_ref.shape[1]

    @pl.when(pl.program_id(1) == 0)
    def _():
        carry_ref[...] = jnp.zeros_like(carry_ref)
        zprev_ref[...] = jnp.zeros_like(zprev_ref)

    h = _rms_norm(x_ref[0], g_ref[...]).astype(BF16)

    p = _dot(h, w_ref[:, SC_OFF:SC_OFF + 3 * BRANCH])
    u = p[:, BRANCH:2 * BRANCH] * p[:, 2 * BRANCH:3 * BRANCH]
    conv = _causal_conv3(u, carry_ref[...], cw_ref[...])
    carry_ref[...] = u[tm - 8:tm]
    ya_ref[0] = (p[:, 0:BRANCH] * conv).astype(BF16)

    q = _dot(h, w_ref[:, SB_OFF:SB_OFF + BRANCH]) * (1.0 / math.sqrt(HEAD_DIM))
    q_ref[0] = q.astype(BF16)
    k = _dot(h, w_ref[:, SB_OFF + BRANCH:SB_OFF + 2 * BRANCH])
    for c in range(tm // SB_BLOCK):
        kt_ref[0, c] = k[c * SB_BLOCK:(c + 1) * SB_BLOCK].T.astype(BF16)
    v_ref[0] = _dot(h, w_ref[:, SB_OFF + 2 * BRANCH:SB_OFF + 3 * BRANCH]).astype(BF16)

    z = _dot(h, w_ref[:, RW_OFF:SG_OFF])
    rw_ref[0] = z + (_shift_rows(z, zprev_ref[...], 1) - z) * mu_ref[...]
    zprev_ref[...] = z[tm - 8:tm]

    z = _gelu_tanh(_dot(h, w_ref[:, SG_OFF:GATE_OFF]))
    u_g = z[:, 0:BRANCH]
    vv = z[:, BRANCH:2 * BRANCH]
    mu = jnp.mean(vv, axis=-1, keepdims=True)
    vc = vv - mu
    var = jnp.mean(vc * vc, axis=-1, keepdims=True)
    vn = (vc * lax.rsqrt(var + LN_EPS) * lng_ref[...] + lnb_ref[...]).astype(BF16)
    tri = (lax.broadcasted_iota(jnp.int32, (SG_CHUNK, SG_CHUNK), 1)
           <= lax.broadcasted_iota(jnp.int32, (SG_CHUNK, SG_CHUNK), 0))
    grp = _head_of_lane((1, BRANCH), 1)
    ws = [jnp.where(tri, sgw_ref[g], 0.0).astype(BF16) for g in range(N_HEADS)]
    for c in range(tm // SG_CHUNK):
        rows = slice(c * SG_CHUNK, (c + 1) * SG_CHUNK)
        vch = vn[rows]
        mixed = sgb_ref[...]
        for g in range(N_HEADS):
            mixed = mixed + jnp.where(grp == g, _dot(ws[g], vch), 0.0)
        yd_ref[0, rows, :] = (u_g[rows] * mixed).astype(BF16)


def _inproj_call(x, g, w, cw, lng, lnb, sgw, sgb, mu):
    bsz, t, d = x.shape
    tm = min(ROW_TILE, t)
    nkb = t // SB_BLOCK
    const2 = lambda b, i: (0, 0)
    tok = lambda width: pl.BlockSpec((1, tm, width), lambda b, i: (b, i, 0))
    return pl.pallas_call(
        _inproj_kernel,
        grid=(bsz, t // tm),
        in_specs=[
            tok(d),
            pl.BlockSpec((1, d), const2),
            pl.BlockSpec((d, GATE_OFF), const2),
            pl.BlockSpec(cw.shape, const2),
            pl.BlockSpec((1, BRANCH), const2),
            pl.BlockSpec((1, BRANCH), const2),
            pl.BlockSpec(sgw.shape, lambda b, i: (0, 0, 0)),
            pl.BlockSpec(sgb.shape, const2),
            pl.BlockSpec(mu.shape, const2),
        ],
        out_specs=[
            tok(BRANCH), tok(BRANCH), tok(BRANCH),
            pl.BlockSpec((1, tm // SB_BLOCK, BRANCH, SB_BLOCK), lambda b, i: (b, i, 0, 0)),
            tok(BRANCH), tok(4 * BRANCH),
        ],
        out_shape=[
            jax.ShapeDtypeStruct((bsz, t, BRANCH), BF16),
            jax.ShapeDtypeStruct((bsz, t, BRANCH), BF16),
            jax.ShapeDtypeStruct((bsz, t, BRANCH), BF16),
            jax.ShapeDtypeStruct((bsz, nkb, BRANCH, SB_BLOCK), BF16),
            jax.ShapeDtypeStruct((bsz, t, BRANCH), BF16),
            jax.ShapeDtypeStruct((bsz, t, 4 * BRANCH), F32),
        ],
        scratch_shapes=[pltpu.VMEM((8, BRANCH), F32), pltpu.VMEM((8, 4 * BRANCH), F32)],
        compiler_params=pltpu.CompilerParams(
            dimension_semantics=("arbitrary", "arbitrary"), vmem_limit_bytes=VMEM_LIMIT),
        name="inproj",
    )(x, g, w, cw, lng, lnb, sgw, sgb, mu)


def _sb_kernel(q_ref, kt_ref, v_ref, o_ref):
    bq = q_ref.shape[1]
    m = N_HEADS * bq
    i = pl.program_id(1)
    q = q_ref[0]
    lane_head = _head_of_lane((1, BRANCH), 1)
    heads = [lane_head == h for h in range(N_HEADS)]
    q4 = jnp.concatenate([jnp.where(hd, q, jnp.zeros_like(q)) for hd in heads], axis=0)
    row = lax.broadcasted_iota(jnp.int32, (m, SB_BLOCK), 0) % bq
    col = lax.broadcasted_iota(jnp.int32, (m, SB_BLOCK), 1)
    below = col < row
    r_i = lax.broadcasted_iota(jnp.int32, (2 * SB_BLOCK, 2 * SB_BLOCK), 0) % SB_BLOCK
    c_i = lax.broadcasted_iota(jnp.int32, (2 * SB_BLOCK, 2 * SB_BLOCK), 1)
    later_all = jnp.logical_or(r_i > c_i, c_i >= SB_BLOCK).astype(BF16)

    def block(j, used, acc, diagonal):
        z = _dot(q4, kt_ref[0, j])
        l = jnp.log(1.0 + jnp.exp(-jnp.abs(z)))
        sp = jnp.maximum(z, 0.0) + l
        log_beta = jnp.minimum(z, 0.0) - l
        if diagonal:
            sp = jnp.where(below, sp, 0.0)
        hi = sp.astype(BF16)
        lo = (sp - hi.astype(F32)).astype(BF16)
        sums = _dot(jnp.concatenate([hi, lo], axis=1), later_all)
        attn = jnp.exp(log_beta - sums[:, :SB_BLOCK] - used)
        if diagonal:
            attn = jnp.where(below, attn, 0.0)
        attn = attn.astype(BF16)
        start = pl.multiple_of(j * SB_BLOCK, SB_BLOCK)
        vj = v_ref[0, pl.ds(start, SB_BLOCK), :]
        attn_wide = jnp.concatenate([attn[h * bq:(h + 1) * bq] for h in range(N_HEADS)], axis=1)
        v4 = jnp.concatenate([jnp.where(hd, vj, jnp.zeros_like(vj)) for hd in heads], axis=0)
        return used + sums[:, SB_BLOCK:], acc + _dot(attn_wide, v4)

    zero = (jnp.zeros((m, SB_BLOCK), F32), jnp.zeros((bq, BRANCH), F32))

    def diagonal_and_two():
        used, acc = block(i, *zero, True)
        used, acc = block(i - 1, used, acc, False)
        used, acc = block(i - 2, used, acc, False)
        return i - 3, used, acc

    def diagonal_only():
        return (i - 1,) + block(i, *zero, True)

    def cond(state):
        j, used, _ = state
        return jnp.logical_and(j >= 0, jnp.min(used) < -SB_LOG_CUTOFF)

    def body(state):
        j, used, acc = state
        used, acc = block(j, used, acc, False)
        return j - 1, used, acc

    _, _, acc = lax.while_loop(cond, body, lax.cond(i >= 2, diagonal_and_two, diagonal_only))
    o_ref[0] = acc.astype(BF16)


def _sb_call(q, kt, v):
    bsz, t, _ = q.shape
    bq = SB_BLOCK
    return pl.pallas_call(
        _sb_kernel,
        grid=(bsz, t // bq),
        in_specs=[
            pl.BlockSpec((1, bq, BRANCH), lambda b, i: (b, i, 0)),
            pl.BlockSpec((1,) + kt.shape[1:], lambda b, i: (b, 0, 0, 0)),
            pl.BlockSpec((1, t, BRANCH), lambda b, i: (b, 0, 0)),
        ],
        out_specs=pl.BlockSpec((1, bq, BRANCH), lambda b, i: (b, i, 0)),
        out_shape=jax.ShapeDtypeStruct((bsz, t, BRANCH), BF16),
        compiler_params=pltpu.CompilerParams(
            dimension_semantics=("arbitrary", "arbitrary"), vmem_limit_bytes=VMEM_LIMIT),
        name="stickbreak",
    )(q, kt, v)


def _rwkv_local(chunks, masks):
    same_head, strict_lc, incl_lc, eye_lc, tril3 = masks
    c = RW_CHUNK
    bf = lambda a: a.astype(BF16)
    tile4 = lambda a: jnp.concatenate([a] * N_HEADS, axis=0)
    expand = lambda a: jnp.where(same_head, tile4(bf(a)), jnp.zeros((), BF16))
    wide = lambda a, b: jnp.concatenate([a, b], axis=1)
    tall = lambda a, b: jnp.concatenate([a, b], axis=0)

    def split3(x):
        hi = bf(x)
        mid = bf(x - hi.astype(F32))
        lo = bf(x - hi.astype(F32) - mid.astype(F32))
        return jnp.concatenate([hi, mid, lo], axis=0)

    st = []
    for r, lw, k2, v, av, bv in chunks:
        cum = _dot(tril3, split3(lw))
        total = cum[c - 1:c]
        e_in = jnp.exp(cum)
        r_in = r * e_in
        a_in = av * e_in * jnp.exp(-lw)
        e_end = jnp.exp(-total)
        e_out = jnp.exp(total - cum)
        st.append(dict(v=v, r_in=r_in, a_in=a_in, b_out=bv * e_out, k_out=k2 * e_out,
                       lhs=bf(tall(a_in * e_end, r_in * e_end))))
    for d in st:
        d["bt4"] = bf(tile4(d["b_out"]).T)
        d["kt4"] = bf(tile4(d["k_out"]).T)
    for d in st:
        gram_b = _dot(d["lhs"], jnp.where(same_head, d["bt4"], jnp.zeros((), BF16)))
        gram_k = _dot(d["lhs"], jnp.where(same_head, d["kt4"], jnp.zeros((), BF16)))
        d["a_ab"] = jnp.where(strict_lc, gram_b[:c], 0.0)
        d["m_rb"] = bf(jnp.where(incl_lc, gram_b[c:], 0.0))
        d["a_ak"] = bf(jnp.where(strict_lc, gram_k[:c], 0.0))
        d["m_rk"] = bf(jnp.where(incl_lc, gram_k[c:], 0.0))
        d["inv"] = eye_lc + d["a_ab"]
        d["pow"] = d["a_ab"]
        d["pow_x"] = expand(d["a_ab"])
    for _ in range(int(math.log2(c)) - 1):
        for d in st:
            d["pow"] = _dot(bf(d["pow"]), d["pow_x"])
            d["pow_x"] = expand(d["pow"])
        for d in st:
            d["inv"] = d["inv"] + _dot(bf(d["inv"]), d["pow_x"])
    for d in st:
        d["v_x"] = expand(d["v"])
        d["akv"] = _dot(d["a_ak"], d["v_x"])
    out = []
    for d in st:
        inv = bf(d["inv"])
        p_x = expand(_dot(inv, expand(d["a_in"])))
        q_x = expand(_dot(inv, expand(d["akv"])))
        qv = tall(q_x, d["v_x"])
        e_mat = d["r_in"] + _dot(d["m_rb"], p_x)
        f_mat = _dot(wide(d["m_rb"], d["m_rk"]), qv)
        g_mat = jnp.where(same_head, _dot(d["bt4"], p_x), 0.0)
        j_mat = jnp.where(same_head, _dot(wide(d["bt4"], d["kt4"]), qv), 0.0)
        out.append((bf(e_mat), f_mat, bf(g_mat), j_mat))
    return out


def _rwkv_kernel(p_ref, w0_ref, w2_ref, a0_ref, a2_ref, g2_ref, kk_ref, ka_ref,
                 rk_ref, gng_ref, gnb_ref, o_ref, state_ref):
    tt = p_ref.shape[1]
    c = RW_CHUNK
    n = N_HEADS * c

    @pl.when(pl.program_id(1) == 0)
    def _():
        state_ref[...] = jnp.zeros_like(state_ref)

    zs = p_ref[0]
    r = zs[:, 0:BRANCH]
    k = zs[:, BRANCH:2 * BRANCH]
    v = zs[:, 2 * BRANCH:3 * BRANCH]
    xwa = zs[:, 3 * BRANCH:3 * BRANCH + 128]
    xg = zs[:, 3 * BRANCH + 128:4 * BRANCH]
    lw = -RW_DECAY_SCALE * jax.nn.sigmoid(
        w0_ref[...] + _dot(jnp.tanh(xwa).astype(BF16), w2_ref[...]))
    a = jax.nn.sigmoid(a0_ref[...] + _dot(xwa.astype(BF16), a2_ref[...]))
    g = _dot(jax.nn.sigmoid(xg).astype(BF16), g2_ref[...])

    r_i = lax.broadcasted_iota(jnp.int32, (n, n), 0)
    c_i = lax.broadcasted_iota(jnp.int32, (n, n), 1)
    same_head = (r_i // HEAD_DIM) == (c_i // HEAD_DIM)
    ones_head2 = jnp.concatenate([same_head.astype(BF16)] * 2, axis=0)

    def head_sum(x):
        hi = x.astype(BF16)
        lo = (x - hi.astype(F32)).astype(BF16)
        return _dot(jnp.concatenate([hi, lo], axis=1), ones_head2)

    kk = k * kk_ref[...]
    kk = kk / jnp.maximum(jnp.sqrt(head_sum(kk * kk)), 1e-12)
    k2 = k * (1.0 + (a - 1.0) * ka_ref[...])
    av = -kk
    bv = kk * a

    t_i = lax.broadcasted_iota(jnp.int32, (c, n), 0)
    s_i = lax.broadcasted_iota(jnp.int32, (c, n), 1) % c
    tril = (lax.broadcasted_iota(jnp.int32, (c, c), 1)
            <= lax.broadcasted_iota(jnp.int32, (c, c), 0)).astype(BF16)
    masks = (same_head, s_i < t_i, s_i <= t_i, (s_i == t_i).astype(F32),
             jnp.concatenate([tril] * 3, axis=1))

    sel = (lax.broadcasted_iota(jnp.int32, (128, tt), 1) // c
           == lax.broadcasted_iota(jnp.int32, (128, tt), 0)).astype(BF16)
    decay_cols = jnp.exp(_dot_split_rhs(sel, lw).T)

    chunks = []
    for j in range(tt // c):
        rows = slice(j * c, (j + 1) * c)
        chunks.append((r[rows], lw[rows], k2[rows], v[rows], av[rows], bv[rows]))
    state = state_ref[...]
    ys = []
    for j, (e_mat, f_mat, g_mat, j_mat) in enumerate(_rwkv_local(chunks, masks)):
        sb = state.astype(BF16)
        ys.append(_dot(e_mat, sb) + f_mat)
        state = state * decay_cols[:, j:j + 1] + _dot(g_mat, sb) + j_mat
    state_ref[...] = state
    y = jnp.concatenate(ys, axis=0)

    inv_n = 1.0 / HEAD_DIM
    yc = y - head_sum(y) * inv_n
    var = head_sum(yc * yc) * inv_n
    yn = yc * lax.rsqrt(var + RW_GN_EPS) * gng_ref[...] + gnb_ref[...]
    bonus = head_sum(r * k2 * rk_ref[...]) * v
    o_ref[0] = ((yn + bonus) * g).astype(BF16)


def _rwkv_call(p, w0, w2p, a0, a2p, g2, k_k, k_a, r_k, gn_g, gn_b):
    bsz, t, width = p.shape
    tt = min(RW_TILE, t)
    const2 = lambda b, i: (0, 0)
    vec = pl.BlockSpec((1, BRANCH), const2)
    return pl.pallas_call(
        _rwkv_kernel,
        grid=(bsz, t // tt),
        in_specs=[
            pl.BlockSpec((1, tt, width), lambda b, i: (b, i, 0)),
            vec, pl.BlockSpec(w2p.shape, const2),
            vec, pl.BlockSpec(a2p.shape, const2),
            pl.BlockSpec(g2.shape, const2),
            vec, vec, vec, vec, vec,
        ],
        out_specs=pl.BlockSpec((1, tt, BRANCH), lambda b, i: (b, i, 0)),
        out_shape=jax.ShapeDtypeStruct((bsz, t, BRANCH), BF16),
        scratch_shapes=[pltpu.VMEM((N_HEADS * RW_CHUNK, BRANCH), F32)],
        compiler_params=pltpu.CompilerParams(
            dimension_semantics=("arbitrary", "arbitrary"), vmem_limit_bytes=VMEM_LIMIT),
        name="rwkv7",
    )(p, w0, w2p, a0, a2p, g2, k_k, k_a, r_k, gn_g, gn_b)


def _merge_kernel(x_ref, g_ref, ya_ref, yb_ref, yc_ref, yd_ref, wg0_ref, wg1_ref, wg2_ref,
                  wg3_ref, gb_ref, wout_ref, wo_ref, o_ref):
    x = x_ref[...]
    h = _rms_norm(x, g_ref[...]).astype(BF16)
    merged = None
    branches = ((ya_ref, wg0_ref), (yb_ref, wg1_ref), (yc_ref, wg2_ref), (yd_ref, wg3_ref))
    for i, (y_ref, wg_ref) in enumerate(branches):
        cols = slice(i * D_MODEL, (i + 1) * D_MODEL)
        gate = jax.nn.sigmoid(_dot(h, wg_ref[...]) + gb_ref[:, cols])
        term = gate * _dot(y_ref[...], wout_ref[i])
        merged = term if merged is None else merged + term
    o_ref[...] = x + _dot(merged.astype(BF16), wo_ref[...])


def _merge_call(x, g, ya, yb, yc, yd, w_in, gb, wout, wo):
    n, d = x.shape
    tm = min(ROW_TILE, n)
    const2 = lambda i: (0, 0)
    tok = lambda width: pl.BlockSpec((tm, width), lambda i: (i, 0))
    return pl.pallas_call(
        _merge_kernel,
        grid=(n // tm,),
        in_specs=[
            tok(d), pl.BlockSpec((1, d), const2),
            tok(BRANCH), tok(BRANCH), tok(BRANCH), tok(BRANCH),
            *[pl.BlockSpec((d, d), lambda i, j=j: (0, GATE_OFF // D_MODEL + j)) for j in range(4)],
            pl.BlockSpec(gb.shape, const2),
            pl.BlockSpec(wout.shape, lambda i: (0, 0, 0)), pl.BlockSpec(wo.shape, const2),
        ],
        out_specs=tok(d),
        out_shape=jax.ShapeDtypeStruct((n, d), F32),
        compiler_params=pltpu.CompilerParams(
            dimension_semantics=("arbitrary",), vmem_limit_bytes=VMEM_LIMIT),
        name="merge",
    )(x, g, ya, yb, yc, yd, w_in, w_in, w_in, w_in, gb, wout, wo)


def _ffn_kernel(x_ref, g_ref, wup_ref, cw_ref, wdn_ref, fg_ref, o_ref, carry_ref, act_ref, *,
                final_norm):
    tm = x_ref.shape[1]

    @pl.when(pl.program_id(1) == 0)
    def _():
        carry_ref[...] = jnp.zeros_like(carry_ref)

    x = x_ref[0]
    h = _rms_norm(x, g_ref[...]).astype(BF16)
    for c in range(D_FF // FF_CHUNK):
        halves = []
        for half, off in enumerate((c * FF_CHUNK, D_FF + c * FF_CHUNK)):
            cols = slice(off, off + FF_CHUNK)
            up = _dot(h, wup_ref[:, cols])
            halves.append(_causal_conv3(up, carry_ref[:, cols], cw_ref[:, cols]))
            carry_ref[:, cols] = up[tm - 8:tm]
        gate, val = halves
        act_ref[:, c * FF_CHUNK:(c + 1) * FF_CHUNK] = (
            gate * jax.nn.sigmoid(gate) * val).astype(BF16)
    out = x + _dot(act_ref[...], wdn_ref[...])
    if final_norm:
        out = _rms_norm(out, fg_ref[...])
    o_ref[0] = out


def _ffn_call(x, g, wup, cw, wdn, fg, final_norm):
    bsz, t, d = x.shape
    tm = min(ROW_TILE, t)
    const2 = lambda b, i: (0, 0)
    tok = pl.BlockSpec((1, tm, d), lambda b, i: (b, i, 0))
    return pl.pallas_call(
        functools.partial(_ffn_kernel, final_norm=final_norm),
        grid=(bsz, t // tm),
        in_specs=[
            tok, pl.BlockSpec((1, d), const2),
            pl.BlockSpec(wup.shape, const2, pipeline_mode=pl.Buffered(1)),
            pl.BlockSpec(cw.shape, const2),
            pl.BlockSpec(wdn.shape, const2, pipeline_mode=pl.Buffered(1)),
            pl.BlockSpec((1, d), const2),
        ],
        out_specs=tok,
        out_shape=jax.ShapeDtypeStruct((bsz, t, d), F32),
        scratch_shapes=[pltpu.VMEM((8, 2 * D_FF), F32), pltpu.VMEM((tm, D_FF), BF16)],
        compiler_params=pltpu.CompilerParams(
            dimension_semantics=("arbitrary", "arbitrary"), vmem_limit_bytes=VMEM_LIMIT),
        name="ffn",
    )(x, g, wup, cw, wdn, fg)


def _layer(x, final_g, last, mix_norm_g, w_in, gate_b, sc_conv_w, sc_out, sb_out, rw_mu, rw_w0,
           rw_w2, rw_a0, rw_a2, rw_g2, rw_k_k, rw_k_a, rw_r_k, rw_gn_g, rw_gn_b, rw_out,
           sg_ln_g, sg_ln_b, sg_w, sg_b, sg_out, w_o, ffn_norm_g, w_up, ffn_conv_w, w_down):
    bsz, t, d = x.shape
    row = lambda a: a.reshape(1, -1)
    w_in_b = w_in.astype(BF16)
    sg_bias = jnp.repeat(sg_b.T, HEAD_DIM, axis=1)
    ya, yd, q, kt, v, p_rw = _inproj_call(
        x, row(mix_norm_g), w_in_b, sc_conv_w, row(sg_ln_g), row(sg_ln_b),
        sg_w, sg_bias, row(rw_mu))
    yb = _sb_call(q, kt, v)
    zeros = jnp.zeros((64, BRANCH), F32)
    w2p = jnp.concatenate([rw_w2, zeros], axis=0).astype(BF16)
    a2p = jnp.concatenate([zeros, rw_a2], axis=0).astype(BF16)
    yc = _rwkv_call(p_rw, row(rw_w0), w2p, row(rw_a0), a2p, rw_g2.astype(BF16),
                    row(rw_k_k), row(rw_k_a), row(rw_r_k), row(rw_gn_g), row(rw_gn_b))
    wout = jnp.stack([sc_out, sb_out, rw_out, sg_out]).astype(BF16)
    flat = lambda a: a.reshape(bsz * t, a.shape[-1])
    x1 = _merge_call(flat(x), row(mix_norm_g), flat(ya), flat(yb), flat(yc), flat(yd),
                     w_in_b, row(gate_b), wout, w_o.astype(BF16))
    return _ffn_call(x1.reshape(bsz, t, d), row(ffn_norm_g), w_up.astype(BF16), ffn_conv_w,
                     w_down.astype(BF16), row(final_g), last)


def kernel(x, mix_norm_g, w_in, gate_b, sc_conv_w, sc_out, sb_out, rw_mu, rw_w0, rw_w2, rw_a0,
           rw_a2, rw_g2, rw_k_k, rw_k_a, rw_r_k, rw_gn_g, rw_gn_b, rw_out, sg_ln_g, sg_ln_b,
           sg_w, sg_b, sg_out, w_o, ffn_norm_g, w_up, ffn_conv_w, w_down, final_norm_g):
    per_layer = (mix_norm_g, w_in, gate_b, sc_conv_w, sc_out, sb_out, rw_mu, rw_w0, rw_w2, rw_a0,
                 rw_a2, rw_g2, rw_k_k, rw_k_a, rw_r_k, rw_gn_g, rw_gn_b, rw_out, sg_ln_g,
                 sg_ln_b, sg_w, sg_b, sg_out, w_o, ffn_norm_g, w_up, ffn_conv_w, w_down)
    depth = w_in.shape[0]
    for l in range(depth):
        x = _layer(x, final_norm_g, l == depth - 1, *(p[l] for p in per_layer))
    return x
```

```python
import functools
import math

import jax
import jax.numpy as jnp
from jax import lax
from jax.experimental import pallas as pl
from jax.experimental.pallas import tpu as pltpu

F32 = jnp.float32
BF16 = jnp.bfloat16

D_MODEL = 1024
HEAD_DIM = 64
N_HEADS = 4
BRANCH = 256
D_FF = 2816
RMS_EPS = 1e-6
LN_EPS = 1e-5
RW_GN_EPS = 64e-5
RW_DECAY_SCALE = math.exp(-0.5)
SG_CHUNK = 128

SC_OFF, SB_OFF, RW_OFF, SG_OFF, GATE_OFF = 0, 768, 1536, 2560, 3072

ROW_TILE = 1024
SB_BLOCK = 128
RW_TILE = 512
RW_CHUNK = 64
FF_CHUNK = 256
CAST_ROWS = 256
SB_LOG_CUTOFF = -104.0
VMEM_LIMIT = 52 * 1024 * 1024


def _dot(a, b):
    return jnp.dot(a, b, preferred_element_type=F32)


def _dot_split_rhs(m, x, passes=3):
    acc = None
    rem = x
    for _ in range(passes):
        hi = rem.astype(BF16)
        d = _dot(m, hi)
        acc = d if acc is None else acc + d
        rem = rem - hi.astype(F32)
    return acc


def _rms_norm(x, g):
    return x * lax.rsqrt(jnp.mean(x * x, axis=-1, keepdims=True) + RMS_EPS) * g


def _shift_rows(u, prev8, s):
    r = pltpu.roll(u, s, 0)
    c = pltpu.roll(prev8, s, 0)
    row = lax.broadcasted_iota(jnp.int32, c.shape, 0)
    top = jnp.where(row < s, c, r[0:8])
    return jnp.concatenate([top, r[8:]], axis=0)


def _causal_conv3(u, prev8, w):
    return (_shift_rows(u, prev8, 2) * w[0:1] + _shift_rows(u, prev8, 1) * w[1:2]
            + u * w[2:3])


def _gelu_tanh(x):
    return 0.5 * x * (1.0 + jnp.tanh(math.sqrt(2.0 / math.pi) * (x + 0.044715 * x * x * x)))


def _head_of_lane(shape, axis):
    return lax.broadcasted_iota(jnp.int32, shape, axis) // HEAD_DIM


def _inproj_kernel(x_ref, g_ref, w_ref, cw_ref, lng_ref, lnb_ref, sgw_ref, sgb_ref,
                   mu_ref, ya_ref, yd_ref, q_ref, kt_ref, v_ref, rw_ref, carry_ref, zprev_ref):
    tm = x_ref.shape[1]

    @pl.when(pl.program_id(1) == 0)
    def _():
        carry_ref[...] = jnp.zeros_like(carry_ref)
        zprev_ref[...] = jnp.zeros_like(zprev_ref)

    h = _rms_norm(x_ref[0], g_ref[...]).astype(BF16)

    p = _dot(h, w_ref[:, SC_OFF:SC_OFF + 3 * BRANCH])
    u = p[:, BRANCH:2 * BRANCH] * p[:, 2 * BRANCH:3 * BRANCH]
    conv = _causal_conv3(u, carry_ref[...], cw_ref[...])
    carry_ref[...] = u[tm - 8:tm]
    ya_ref[0] = (p[:, 0:BRANCH] * conv).astype(BF16)

    q = _dot(h, w_ref[:, SB_OFF:SB_OFF + BRANCH]) * (1.0 / math.sqrt(HEAD_DIM))
    q_ref[0] = q.astype(BF16)
    k = _dot(h, w_ref[:, SB_OFF + BRANCH:SB_OFF + 2 * BRANCH])
    for c in range(tm // SB_BLOCK):
        kt_ref[0, c] = k[c * SB_BLOCK:(c + 1) * SB_BLOCK].T.astype(BF16)
    v_ref[0] = _dot(h, w_ref[:, SB_OFF + 2 * BRANCH:SB_OFF + 3 * BRANCH]).astype(BF16)

    z = _dot(h, w_ref[:, RW_OFF:SG_OFF])
    rw_ref[0] = z + (_shift_rows(z, zprev_ref[...], 1) - z) * mu_ref[...]
    zprev_ref[...] = z[tm - 8:tm]

    z = _gelu_tanh(_dot(h, w_ref[:, SG_OFF:GATE_OFF]))
    u_g = z[:, 0:BRANCH]
    vv = z[:, BRANCH:2 * BRANCH]
    mu = jnp.mean(vv, axis=-1, keepdims=True)
    vc = vv - mu
    var = jnp.mean(vc * vc, axis=-1, keepdims=True)
    vn = (vc * lax.rsqrt(var + LN_EPS) * lng_ref[...] + lnb_ref[...]).astype(BF16)
    tri = (lax.broadcasted_iota(jnp.int32, (SG_CHUNK, SG_CHUNK), 1)
           <= lax.broadcasted_iota(jnp.int32, (SG_CHUNK, SG_CHUNK), 0))
    grp = _head_of_lane((1, BRANCH), 1)
    ws = [jnp.where(tri, sgw_ref[g], 0.0).astype(BF16) for g in range(N_HEADS)]
    for c in range(tm // SG_CHUNK):
        rows = slice(c * SG_CHUNK, (c + 1) * SG_CHUNK)
        vch = vn[rows]
        mixed = sgb_ref[...]
        for g in range(N_HEADS):
            mixed = mixed + jnp.where(grp == g, _dot(ws[g], vch), 0.0)
        yd_ref[0, rows, :] = (u_g[rows] * mixed).astype(BF16)


def _inproj_call(x, g, w, cw, lng, lnb, sgw, sgb, mu):
    bsz, t, d = x.shape
    tm = min(ROW_TILE, t)
    nkb = t // SB_BLOCK
    const2 = lambda b, i: (0, 0)
    tok = lambda width: pl.BlockSpec((1, tm, width), lambda b, i: (b, i, 0))
    return pl.pallas_call(
        _inproj_kernel,
        grid=(bsz, t // tm),
        in_specs=[
            tok(d),
            pl.BlockSpec((1, d), const2),
            pl.BlockSpec((d, GATE_OFF), const2),
            pl.BlockSpec(cw.shape, const2),
            pl.BlockSpec((1, BRANCH), const2),
            pl.BlockSpec((1, BRANCH), const2),
            pl.BlockSpec(sgw.shape, lambda b, i: (0, 0, 0)),
            pl.BlockSpec(sgb.shape, const2),
            pl.BlockSpec(mu.shape, const2),
        ],
        out_specs=[
            tok(BRANCH), tok(BRANCH), tok(BRANCH),
            pl.BlockSpec((1, tm // SB_BLOCK, BRANCH, SB_BLOCK), lambda b, i: (b, i, 0, 0)),
            tok(BRANCH), tok(4 * BRANCH),
        ],
        out_shape=[
            jax.ShapeDtypeStruct((bsz, t, BRANCH), BF16),
            jax.ShapeDtypeStruct((bsz, t, BRANCH), BF16),
            jax.ShapeDtypeStruct((bsz, t, BRANCH), BF16),
            jax.ShapeDtypeStruct((bsz, nkb, BRANCH, SB_BLOCK), BF16),
            jax.ShapeDtypeStruct((bsz, t, BRANCH), BF16),
            jax.ShapeDtypeStruct((bsz, t, 4 * BRANCH), F32),
        ],
        scratch_shapes=[pltpu.VMEM((8, BRANCH), F32), pltpu.VMEM((8, 4 * BRANCH), F32)],
        compiler_params=pltpu.CompilerParams(
            dimension_semantics=("arbitrary", "arbitrary"), vmem_limit_bytes=VMEM_LIMIT),
        name="inproj",
    )(x, g, w, cw, lng, lnb, sgw, sgb, mu)


def _sb_kernel(q_ref, kt_ref, v_ref, o_ref):
    bq = q_ref.shape[1]
    m = N_HEADS * bq
    i = pl.program_id(1)
    q = q_ref[0]
    lane_head = _head_of_lane((1, BRANCH), 1)
    heads = [lane_head == h for h in range(N_HEADS)]
    q4 = jnp.concatenate([jnp.where(hd, q, jnp.zeros_like(q)) for hd in heads], axis=0)
    row = lax.broadcasted_iota(jnp.int32, (m, SB_BLOCK), 0) % bq
    col = lax.broadcasted_iota(jnp.int32, (m, SB_BLOCK), 1)
    below = col < row
    r_i = lax.broadcasted_iota(jnp.int32, (2 * SB_BLOCK, 2 * SB_BLOCK), 0) % SB_BLOCK
    c_i = lax.broadcasted_iota(jnp.int32, (2 * SB_BLOCK, 2 * SB_BLOCK), 1)
    later_all = jnp.logical_or(r_i > c_i, c_i >= SB_BLOCK).astype(BF16)

    def block(j, used, acc, diagonal):
        z = _dot(q4, kt_ref[0, j])
        l = jnp.log(1.0 + jnp.exp(-jnp.abs(z)))
        sp = jnp.maximum(z, 0.0) + l
        log_beta = jnp.minimum(z, 0.0) - l
        if diagonal:
            sp = jnp.where(below, sp, 0.0)
        hi = sp.astype(BF16)
        lo = (sp - hi.astype(F32)).astype(BF16)
        sums = _dot(jnp.concatenate([hi, lo], axis=1), later_all)
        attn = jnp.exp(log_beta - sums[:, :SB_BLOCK] - used)
        if diagonal:
            attn = jnp.where(below, attn, 0.0)
        attn = attn.astype(BF16)
        start = pl.multiple_of(j * SB_BLOCK, SB_BLOCK)
        vj = v_ref[0, pl.ds(start, SB_BLOCK), :]
        attn_wide = jnp.concatenate([attn[h * bq:(h + 1) * bq] for h in range(N_HEADS)], axis=1)
        v4 = jnp.concatenate([jnp.where(hd, vj, jnp.zeros_like(vj)) for hd in heads], axis=0)
        return used + sums[:, SB_BLOCK:], acc + _dot(attn_wide, v4)

    zero = (jnp.zeros((m, SB_BLOCK), F32), jnp.zeros((bq, BRANCH), F32))

    def diagonal_and_two():
        used, acc = block(i, *zero, True)
        used, acc = block(i - 1, used, acc, False)
        used, acc = block(i - 2, used, acc, False)
        return i - 3, used, acc

    def diagonal_only():
        return (i - 1,) + block(i, *zero, True)

    def cond(state):
        j, used, _ = state
        return jnp.logical_and(j >= 0, jnp.min(used) < -SB_LOG_CUTOFF)

    def body(state):
        j, used, acc = state
        used, acc = block(j, used, acc, False)
        return j - 1, used, acc

    _, _, acc = lax.while_loop(cond, body, lax.cond(i >= 2, diagonal_and_two, diagonal_only))
    o_ref[0] = acc.astype(BF16)


def _sb_call(q, kt, v):
    bsz, t, _ = q.shape
    bq = SB_BLOCK
    return pl.pallas_call(
        _sb_kernel,
        grid=(bsz, t // bq),
        in_specs=[
            pl.BlockSpec((1, bq, BRANCH), lambda b, i: (b, i, 0)),
            pl.BlockSpec((1,) + kt.shape[1:], lambda b, i: (b, 0, 0, 0)),
            pl.BlockSpec((1, t, BRANCH), lambda b, i: (b, 0, 0)),
        ],
        out_specs=pl.BlockSpec((1, bq, BRANCH), lambda b, i: (b, i, 0)),
        out_shape=jax.ShapeDtypeStruct((bsz, t, BRANCH), BF16),
        compiler_params=pltpu.CompilerParams(
            dimension_semantics=("arbitrary", "arbitrary"), vmem_limit_bytes=VMEM_LIMIT),
        name="stickbreak",
    )(q, kt, v)


def _rwkv_local(chunks, masks):
    same_head, strict_lc, incl_lc, eye_lc, tril3 = masks
    c = RW_CHUNK
    bf = lambda a: a.astype(BF16)
    tile4 = lambda a: jnp.concatenate([a] * N_HEADS, axis=0)
    expand = lambda a: jnp.where(same_head, tile4(bf(a)), jnp.zeros((), BF16))
    wide = lambda a, b: jnp.concatenate([a, b], axis=1)
    tall = lambda a, b: jnp.concatenate([a, b], axis=0)

    def split3(x):
        hi = bf(x)
        mid = bf(x - hi.astype(F32))
        lo = bf(x - hi.astype(F32) - mid.astype(F32))
        return jnp.concatenate([hi, mid, lo], axis=0)

    st = []
    for r, lw, k2, v, av, bv in chunks:
        cum = _dot(tril3, split3(lw))
        total = cum[c - 1:c]
        e_in = jnp.exp(cum)
        r_in = r * e_in
        a_in = av * e_in * jnp.exp(-lw)
        e_end = jnp.exp(-total)
        e_out = jnp.exp(total - cum)
        st.append(dict(v=v, r_in=r_in, a_in=a_in, b_out=bv * e_out, k_out=k2 * e_out,
                       lhs=bf(tall(a_in * e_end, r_in * e_end))))
    for d in st:
        d["bt4"] = bf(tile4(d["b_out"]).T)
        d["kt4"] = bf(tile4(d["k_out"]).T)
    for d in st:
        gram_b = _dot(d["lhs"], jnp.where(same_head, d["bt4"], jnp.zeros((), BF16)))
        gram_k = _dot(d["lhs"], jnp.where(same_head, d["kt4"], jnp.zeros((), BF16)))
        d["a_ab"] = jnp.where(strict_lc, gram_b[:c], 0.0)
        d["m_rb"] = bf(jnp.where(incl_lc, gram_b[c:], 0.0))
        d["a_ak"] = bf(jnp.where(strict_lc, gram_k[:c], 0.0))
        d["m_rk"] = bf(jnp.where(incl_lc, gram_k[c:], 0.0))
        d["inv"] = eye_lc + d["a_ab"]
        d["pow"] = d["a_ab"]
        d["pow_x"] = expand(d["a_ab"])
    for _ in range(int(math.log2(c)) - 1):
        for d in st:
            d["pow"] = _dot(bf(d["pow"]), d["pow_x"])
            d["pow_x"] = expand(d["pow"])
        for d in st:
            d["inv"] = d["inv"] + _dot(bf(d["inv"]), d["pow_x"])
    for d in st:
        d["v_x"] = expand(d["v"])
        d["akv"] = _dot(d["a_ak"], d["v_x"])
    out = []
    for d in st:
        inv = bf(d["inv"])
        p_x = expand(_dot(inv, expand(d["a_in"])))
        q_x = expand(_dot(inv, expand(d["akv"])))
        qv = tall(q_x, d["v_x"])
        e_mat = d["r_in"] + _dot(d["m_rb"], p_x)
        f_mat = _dot(wide(d["m_rb"], d["m_rk"]), qv)
        g_mat = jnp.where(same_head, _dot(d["bt4"], p_x), 0.0)
        j_mat = jnp.where(same_head, _dot(wide(d["bt4"], d["kt4"]), qv), 0.0)
        out.append((bf(e_mat), f_mat, bf(g_mat), j_mat))
    return out


def _rwkv_kernel(p_ref, w0_ref, w2_ref, a0_ref, a2_ref, g2_ref, kk_ref, ka_ref,
                 rk_ref, gng_ref, gnb_ref, o_ref, state_ref):
    tt = p_ref.shape[1]
    c = RW_CHUNK
    n = N_HEADS * c

    @pl.when(pl.program_id(1) == 0)
    def _():
        state_ref[...] = jnp.zeros_like(state_ref)

    zs = p_ref[0]
    r = zs[:, 0:BRANCH]
    k = zs[:, BRANCH:2 * BRANCH]
    v = zs[:, 2 * BRANCH:3 * BRANCH]
    xwa = zs[:, 3 * BRANCH:3 * BRANCH + 128]
    xg = zs[:, 3 * BRANCH + 128:4 * BRANCH]
    lw = -RW_DECAY_SCALE * jax.nn.sigmoid(
        w0_ref[...] + _dot(jnp.tanh(xwa).astype(BF16), w2_ref[...]))
    a = jax.nn.sigmoid(a0_ref[...] + _dot(xwa.astype(BF16), a2_ref[...]))
    g = _dot(jax.nn.sigmoid(xg).astype(BF16), g2_ref[...])

    r_i = lax.broadcasted_iota(jnp.int32, (n, n), 0)
    c_i = lax.broadcasted_iota(jnp.int32, (n, n), 1)
    same_head = (r_i // HEAD_DIM) == (c_i // HEAD_DIM)
    ones_head2 = jnp.concatenate([same_head.astype(BF16)] * 2, axis=0)

    def head_sum(x):
        hi = x.astype(BF16)
        lo = (x - hi.astype(F32)).astype(BF16)
        return _dot(jnp.concatenate([hi, lo], axis=1), ones_head2)

    kk = k * kk_ref[...]
    kk = kk / jnp.maximum(jnp.sqrt(head_sum(kk * kk)), 1e-12)
    k2 = k * (1.0 + (a - 1.0) * ka_ref[...])
    av = -kk
    bv = kk * a

    t_i = lax.broadcasted_iota(jnp.int32, (c, n), 0)
    s_i = lax.broadcasted_iota(jnp.int32, (c, n), 1) % c
    tril = (lax.broadcasted_iota(jnp.int32, (c, c), 1)
            <= lax.broadcasted_iota(jnp.int32, (c, c), 0)).astype(BF16)
    masks = (same_head, s_i < t_i, s_i <= t_i, (s_i == t_i).astype(F32),
             jnp.concatenate([tril] * 3, axis=1))

    sel = (lax.broadcasted_iota(jnp.int32, (128, tt), 1) // c
           == lax.broadcasted_iota(jnp.int32, (128, tt), 0)).astype(BF16)
    decay_cols = jnp.exp(_dot_split_rhs(sel, lw).T)

    chunks = []
    for j in range(tt // c):
        rows = slice(j * c, (j + 1) * c)
        chunks.append((r[rows], lw[rows], k2[rows], v[rows], av[rows], bv[rows]))
    state = state_ref[...]
    ys = []
    for j, (e_mat, f_mat, g_mat, j_mat) in enumerate(_rwkv_local(chunks, masks)):
        sb = state.astype(BF16)
        ys.append(_dot(e_mat, sb) + f_mat)
        state = state * decay_cols[:, j:j + 1] + _dot(g_mat, sb) + j_mat
    state_ref[...] = state
    y = jnp.concatenate(ys, axis=0)

    inv_n = 1.0 / HEAD_DIM
    yc = y - head_sum(y) * inv_n
    var = head_sum(yc * yc) * inv_n
    yn = yc * lax.rsqrt(var + RW_GN_EPS) * gng_ref[...] + gnb_ref[...]
    bonus = head_sum(r * k2 * rk_ref[...]) * v
    o_ref[0] = ((yn + bonus) * g).astype(BF16)


def _rwkv_call(p, w0, w2p, a0, a2p, g2, k_k, k_a, r_k, gn_g, gn_b):
    bsz, t, width = p.shape
    tt = min(RW_TILE, t)
    const2 = lambda b, i: (0, 0)
    vec = pl.BlockSpec((1, BRANCH), const2)
    return pl.pallas_call(
        _rwkv_kernel,
        grid=(bsz, t // tt),
        in_specs=[
            pl.BlockSpec((1, tt, width), lambda b, i: (b, i, 0)),
            vec, pl.BlockSpec(w2p.shape, const2),
            vec, pl.BlockSpec(a2p.shape, const2),
            pl.BlockSpec(g2.shape, const2),
            vec, vec, vec, vec, vec,
        ],
        out_specs=pl.BlockSpec((1, tt, BRANCH), lambda b, i: (b, i, 0)),
        out_shape=jax.ShapeDtypeStruct((bsz, t, BRANCH), BF16),
        scratch_shapes=[pltpu.VMEM((N_HEADS * RW_CHUNK, BRANCH), F32)],
        compiler_params=pltpu.CompilerParams(
            dimension_semantics=("arbitrary", "arbitrary"), vmem_limit_bytes=VMEM_LIMIT),
        name="rwkv7",
    )(p, w0, w2p, a0, a2p, g2, k_k, k_a, r_k, gn_g, gn_b)


def _merge_kernel(x_ref, g_ref, ya_ref, yb_ref, yc_ref, yd_ref, wg0_ref, wg1_ref, wg2_ref,
                  wg3_ref, gb_ref, wout_ref, wo_ref, o_ref):
    x = x_ref[...]
    h = _rms_norm(x, g_ref[...]).astype(BF16)
    merged = None
    branches = ((ya_ref, wg0_ref), (yb_ref, wg1_ref), (yc_ref, wg2_ref), (yd_ref, wg3_ref))
    for i, (y_ref, wg_ref) in enumerate(branches):
        cols = slice(i * D_MODEL, (i + 1) * D_MODEL)
        gate = jax.nn.sigmoid(_dot(h, wg_ref[...]) + gb_ref[:, cols])
        term = gate * _dot(y_ref[...], wout_ref[i])
        merged = term if merged is None else merged + term
    o_ref[...] = x + _dot(merged.astype(BF16), wo_ref[...])


def _merge_call(x, g, ya, yb, yc, yd, w_in, gb, wout, wo):
    n, d = x.shape
    tm = min(ROW_TILE, n)
    const2 = lambda i: (0, 0)
    tok = lambda width: pl.BlockSpec((tm, width), lambda i: (i, 0))
    return pl.pallas_call(
        _merge_kernel,
        grid=(n // tm,),
        in_specs=[
            tok(d), pl.BlockSpec((1, d), const2),
            tok(BRANCH), tok(BRANCH), tok(BRANCH), tok(BRANCH),
            *[pl.BlockSpec((d, d), lambda i, j=j: (0, GATE_OFF // D_MODEL + j)) for j in range(4)],
            pl.BlockSpec(gb.shape, const2),
            pl.BlockSpec(wout.shape, lambda i: (0, 0, 0)), pl.BlockSpec(wo.shape, const2),
        ],
        out_specs=tok(d),
        out_shape=jax.ShapeDtypeStruct((n, d), F32),
        compiler_params=pltpu.CompilerParams(
            dimension_semantics=("arbitrary",), vmem_limit_bytes=VMEM_LIMIT),
        name="merge",
    )(x, g, ya, yb, yc, yd, w_in, w_in, w_in, w_in, gb, wout, wo)


def _ffn_kernel(x_ref, g_ref, wup_ref, cw_ref, wdn_ref, fg_ref, o_ref, carry_ref, act_ref, *,
                final_norm):
    tm = x_ref.shape[1]

    @pl.when(pl.program_id(1) == 0)
    def _():
        carry_ref[...] = jnp.zeros_like(carry_ref)

    x = x_ref[0]
    h = _rms_norm(x, g_ref[...]).astype(BF16)
    for c in range(D_FF // FF_CHUNK):
        halves = []
        for half, off in enumerate((c * FF_CHUNK, D_FF + c * FF_CHUNK)):
            cols = slice(off, off + FF_CHUNK)
            up = _dot(h, wup_ref[:, cols])
            halves.append(_causal_conv3(up, carry_ref[:, cols], cw_ref[:, cols]))
            carry_ref[:, cols] = up[tm - 8:tm]
        gate, val = halves
        act_ref[:, c * FF_CHUNK:(c + 1) * FF_CHUNK] = (
            gate * jax.nn.sigmoid(gate) * val).astype(BF16)
    out = x + _dot(act_ref[...], wdn_ref[...])
    if final_norm:
        out = _rms_norm(out, fg_ref[...])
    o_ref[0] = out


def _ffn_call(x, g, wup, cw, wdn, fg, final_norm):
    bsz, t, d = x.shape
    tm = min(ROW_TILE, t)
    const2 = lambda b, i: (0, 0)
    tok = pl.BlockSpec((1, tm, d), lambda b, i: (b, i, 0))
    return pl.pallas_call(
        functools.partial(_ffn_kernel, final_norm=final_norm),
        grid=(bsz, t // tm),
        in_specs=[
            tok, pl.BlockSpec((1, d), const2),
            pl.BlockSpec(wup.shape, const2, pipeline_mode=pl.Buffered(1)),
            pl.BlockSpec(cw.shape, const2),
            pl.BlockSpec(wdn.shape, const2, pipeline_mode=pl.Buffered(1)),
            pl.BlockSpec((1, d), const2),
        ],
        out_specs=tok,
        out_shape=jax.ShapeDtypeStruct((bsz, t, d), F32),
        scratch_shapes=[pltpu.VMEM((8, 2 * D_FF), F32), pltpu.VMEM((tm, D_FF), BF16)],
        compiler_params=pltpu.CompilerParams(
            dimension_semantics=("arbitrary", "arbitrary"), vmem_limit_bytes=VMEM_LIMIT),
        name="ffn",
    )(x, g, wup, cw, wdn, fg)


def _cast_kernel(w_ref, o_ref):
    o_ref[...] = w_ref[0].astype(BF16)


def _to_bf16(w, l):
    _, r, c = w.shape
    br = CAST_ROWS
    return pl.pallas_call(
        _cast_kernel,
        grid=(r // br,),
        in_specs=[pl.BlockSpec((1, br, c), lambda i: (l, i, 0))],
        out_specs=pl.BlockSpec((br, c), lambda i: (i, 0)),
        out_shape=jax.ShapeDtypeStruct((r, c), BF16),
        compiler_params=pltpu.CompilerParams(
            dimension_semantics=("arbitrary",), vmem_limit_bytes=VMEM_LIMIT),
        name="to_bf16",
    )(w)


def _layer(x, final_g, last, mix_norm_g, w_in, gate_b, sc_conv_w, sc_out, sb_out, rw_mu, rw_w0,
           rw_w2, rw_a0, rw_a2, rw_g2, rw_k_k, rw_k_a, rw_r_k, rw_gn_g, rw_gn_b, rw_out,
           sg_ln_g, sg_ln_b, sg_w, sg_b, sg_out, w_o, ffn_norm_g, w_up, ffn_conv_w, w_down):
    bsz, t, d = x.shape
    row = lambda a: a.reshape(1, -1)
    w_in_b = w_in.astype(BF16)
    sg_bias = jnp.repeat(sg_b.T, HEAD_DIM, axis=1)
    ya, yd, q, kt, v, p_rw = _inproj_call(
        x, row(mix_norm_g), w_in_b, sc_conv_w, row(sg_ln_g), row(sg_ln_b),
        sg_w, sg_bias, row(rw_mu))
    yb = _sb_call(q, kt, v)
    zeros = jnp.zeros((64, BRANCH), F32)
    w2p = jnp.concatenate([rw_w2, zeros], axis=0).astype(BF16)
    a2p = jnp.concatenate([zeros, rw_a2], axis=0).astype(BF16)
    yc = _rwkv_call(p_rw, row(rw_w0), w2p, row(rw_a0), a2p, rw_g2.astype(BF16),
                    row(rw_k_k), row(rw_k_a), row(rw_r_k), row(rw_gn_g), row(rw_gn_b))
    wout = jnp.stack([sc_out, sb_out, rw_out, sg_out]).astype(BF16)
    flat = lambda a: a.reshape(bsz * t, a.shape[-1])
    x1 = _merge_call(flat(x), row(mix_norm_g), flat(ya), flat(yb), flat(yc), flat(yd),
                     w_in_b, row(gate_b), wout, w_o.astype(BF16))
    return _ffn_call(x1.reshape(bsz, t, d), row(ffn_norm_g), w_up.astype(BF16), ffn_conv_w,
                     w_down.astype(BF16), row(final_g), last)


def kernel(x, mix_norm_g, w_in, gate_b, sc_conv_w, sc_out, sb_out, rw_mu, rw_w0, rw_w2, rw_a0,
           rw_a2, rw_g2, rw_k_k, rw_k_a, rw_r_k, rw_gn_g, rw_gn_b, rw_out, sg_ln_g, sg_ln_b,
           sg_w, sg_b, sg_out, w_o, ffn_norm_g, w_up, ffn_conv_w, w_down, final_norm_g):
    per_layer = (mix_norm_g, w_in, gate_b, sc_conv_w, sc_out, sb_out, rw_mu, rw_w0, rw_w2, rw_a0,
                 rw_a2, rw_g2, rw_k_k, rw_k_a, rw_r_k, rw_gn_g, rw_gn_b, rw_out, sg_ln_g,
                 sg_ln_b, sg_w, sg_b, sg_out, w_o, ffn_norm_g, w_up, ffn_conv_w, w_down)
    big = (w_in, w_o, w_up, w_down)
    depth = w_in.shape[0]
    for l in range(depth):
        args = [_to_bf16(p, l) if any(p is b for b in big) else p[l] for p in per_layer]
        x = _layer(x, final_norm_g, l == depth - 1, *args)
    return x
```
